```python
import math
import jax, jax.numpy as jnp
from jax import lax
import numpy as np

D_MODEL = 1024
BATCH = 16
SEQ = 4096
DEPTH = 2
DEC_BATCH = 8
DEC_SEQ = 4096
PAST_LEN = 128

N_META = 16
GRID_W = 64
N_HEADS = 8
N_KV_HEADS = 2
HEAD_DIM = 64
ATTN_WIDTH = N_HEADS * HEAD_DIM
KV_WIDTH = N_KV_HEADS * HEAD_DIM
ATTN_IN = ATTN_WIDTH + 2 * KV_WIDTH
Q_BLOCK = 128
ROPE_THETA = 10000.0
ROPE_PAIRS = HEAD_DIM // 2
ROPE_PAIRS_PER_AXIS = ROPE_PAIRS // 2
HYENA_WIDTH = D_MODEL - ATTN_WIDTH
HY_GROUPS = 8
HY_GROUP_DIM = HYENA_WIDTH // HY_GROUPS
HY_SHORT = 3
HY_BANDS = 16
HY_EMB = 1 + 2 * HY_BANDS
HY_FH = 64
IN_WIDTH = ATTN_IN + 3 * HYENA_WIDTH
PEER_HEADS = 8
PEER_NKEYS = 128
PEER_EXPERTS = PEER_NKEYS * PEER_NKEYS
PEER_TOPK = 16
PEER_DKEY = 256
PEER_DHALF = PEER_DKEY // 2
PEER_CHUNK = 256
NORM_EPS = 1e-6

kernel_name = "hymba_attn_hyena_peer_encoder"


def _rmsnorm(x, g):
    xf = x.astype(jnp.float32)
    y = xf * lax.rsqrt(jnp.mean(xf * xf, axis=-1, keepdims=True) + NORM_EPS)
    return (y * g.astype(jnp.float32)).astype(x.dtype)


def _group_rmsnorm(x, g):
    shp = x.shape
    return _rmsnorm(x.reshape(shp[:-1] + g.shape), g).reshape(shp)


def _axial_rope(n_tok):
    rows = n_tok // GRID_W
    row = jnp.repeat(jnp.arange(rows, dtype=jnp.float32), GRID_W)
    col = jnp.tile(jnp.arange(GRID_W, dtype=jnp.float32), rows)
    inv = ROPE_THETA ** (-jnp.arange(ROPE_PAIRS_PER_AXIS, dtype=jnp.float32) / ROPE_PAIRS_PER_AXIS)
    ang = jnp.concatenate([row[:, None] * inv, col[:, None] * inv], axis=-1)
    ang = jnp.concatenate([jnp.zeros((N_META, ROPE_PAIRS), jnp.float32), ang], axis=0)
    return jnp.cos(ang), jnp.sin(ang)


def _apply_rope(x, cos, sin):
    xr = x.reshape(x.shape[:-1] + (ROPE_PAIRS, 2))
    x0, x1 = xr[..., 0], xr[..., 1]
    c = cos[None, :, None, :]
    s = sin[None, :, None, :]
    return jnp.stack([x0 * c - x1 * s, x0 * s + x1 * c], axis=-1).reshape(x.shape)


def _attention_branch(a_in, q_norm_g, k_norm_g, cos, sin):
    B, L, _ = a_in.shape
    dt = a_in.dtype
    q, k, v = jnp.split(a_in, [ATTN_WIDTH, ATTN_WIDTH + KV_WIDTH], axis=-1)
    q = q.reshape(B, L, N_HEADS, HEAD_DIM).astype(jnp.float32)
    k = k.reshape(B, L, N_KV_HEADS, HEAD_DIM).astype(jnp.float32)
    v = v.reshape(B, L, N_KV_HEADS, HEAD_DIM)
    q = _apply_rope(_rmsnorm(q, q_norm_g), cos, sin).astype(dt)
    k = _apply_rope(_rmsnorm(k, k_norm_g), cos, sin).astype(dt)
    G = N_HEADS // N_KV_HEADS
    scale = HEAD_DIM ** -0.5
    qg = q.reshape(B, L, N_KV_HEADS, G, HEAD_DIM)

    def block(qb):
        s = jnp.einsum('bqkgd,bskd->bkgqs', qb, k, preferred_element_type=jnp.float32) * scale
        p = jax.nn.softmax(s, axis=-1)
        return jnp.einsum('bkgqs,bskd->bqkgd', p.astype(v.dtype), v)

    o_meta = block(qg[:, :N_META]).reshape(B, N_META, ATTN_WIDTH)
    S = L - N_META
    nb = S // Q_BLOCK
    qr = qg[:, N_META:].reshape(B, nb, Q_BLOCK, N_KV_HEADS, G, HEAD_DIM).transpose(1, 0, 2, 3, 4, 5)
    o_real = lax.map(block, qr)
    o_real = o_real.transpose(1, 0, 2, 3, 4, 5).reshape(B, S, ATTN_WIDTH)
    return jnp.concatenate([o_meta, o_real], axis=1)


def _hyena_filter(L, w1, b1, sin_freq, w2, b2, w3, decay):
    f32 = jnp.float32
    t = jnp.arange(L, dtype=f32)
    tn = t / (L - 1)
    bands = jnp.linspace(1e-4, HY_BANDS - 1, HY_BANDS, dtype=f32)
    w = (2.0 * math.pi / L) * t
    z = jnp.concatenate([tn[:, None], jnp.cos(w[:, None] * bands), jnp.sin(w[:, None] * bands)], axis=-1)
    sf = sin_freq.astype(f32)
    h = jnp.sin(sf[0] * (z @ w1.astype(f32) + b1.astype(f32)))
    h = jnp.sin(sf[1] * (h @ w2.astype(f32) + b2.astype(f32)))
    h = (h @ w3.astype(f32)).reshape(L, 2, HYENA_WIDTH)
    h = h * jnp.exp(-tn[:, None, None] * decay.astype(f32)[None])
    hf, hb = h[:, 0], h[:, 1]
    return jnp.concatenate([hf, jnp.zeros((1, HYENA_WIDTH), f32), hb[:0:-1]], axis=0)


def _hyena_branch(z_in, conv_w, conv_b, filt, dskip):
    dt = z_in.dtype
    zp = jnp.pad(z_in, ((0, 0), (1, 1), (0, 0)))
    z = zp[:, :-2] * conv_w[0] + zp[:, 1:-1] * conv_w[1] + zp[:, 2:] * conv_w[2] + conv_b
    x0, x1, vv = jnp.split(z, 3, axis=-1)
    u = (vv * x0).astype(jnp.float32)
    L = u.shape[1]
    U = jnp.fft.rfft(u, n=2 * L, axis=1)
    Hf = jnp.fft.rfft(filt, axis=0)
    y = jnp.fft.irfft(U * Hf[None], n=2 * L, axis=1)[:, :L]
    y = y + u * dskip.astype(jnp.float32)
    return (y * x1.astype(jnp.float32)).astype(dt)


def _peer(x, w_q, sub_keys, u_tab, v_tab):
    T = x.shape[0]
    pad = (-T) % PEER_CHUNK
    xc = jnp.pad(x, ((0, pad), (0, 0))).reshape(-1, PEER_CHUNK, D_MODEL)

    def chunk(xb):
        q = (xb @ w_q).reshape(PEER_CHUNK, PEER_HEADS, 2, PEER_DHALF)
        s = jnp.einsum('chpd,phnd->chpn', q, sub_keys, preferred_element_type=jnp.float32)
        s1, i1 = lax.top_k(s[:, :, 0], PEER_TOPK)
        s2, i2 = lax.top_k(s[:, :, 1], PEER_TOPK)
        cand = (s1[..., :, None] + s2[..., None, :]).reshape(PEER_CHUNK, PEER_HEADS, PEER_TOPK * PEER_TOPK)
        cidx = (i1[..., :, None] * PEER_NKEYS + i2[..., None, :]).reshape(PEER_CHUNK, PEER_HEADS, PEER_TOPK * PEER_TOPK)
        top_s, top_j = lax.top_k(cand, PEER_TOPK)
        eidx = jnp.take_along_axis(cidx, top_j, axis=-1)
        g = jax.nn.softmax(top_s, axis=-1)
        ue = jnp.take(u_tab, eidx, axis=0)
        ve = jnp.take(v_tab, eidx, axis=0)
        a = jax.nn.gelu(jnp.einsum('chkd,cd->chk', ue, xb, preferred_element_type=jnp.float32), approximate=False)
        return jnp.einsum('chk,chkd->cd', (g * a).astype(xb.dtype), ve)

    out = lax.map(chunk, xc).reshape(-1, D_MODEL)
    return out[:T]


def _encode(x, meta_tokens, norm1_g, w_in, q_norm_g, k_norm_g, hy_conv_w, hy_conv_b,
            hy_ffn_w1, hy_ffn_b1, hy_sin_freq, hy_ffn_w2, hy_ffn_b2, hy_ffn_w3, hy_decay, hy_dskip,
            attn_out_g, hy_out_g, w_out, norm2_g, peer_wq, peer_keys, peer_u, peer_v, final_g):
    B, S, _ = x.shape
    L = S + N_META
    dt = x.dtype
    h = jnp.concatenate([jnp.broadcast_to(meta_tokens.astype(dt)[None], (B, N_META, D_MODEL)), x], axis=1)
    cos, sin = _axial_rope(S)
    for l in range(DEPTH):
        hn = _rmsnorm(h, norm1_g[l])
        proj = hn @ w_in[l]
        a_in, z_in = proj[..., :ATTN_IN], proj[..., ATTN_IN:]
        o_att = _attention_branch(a_in, q_norm_g[l], k_norm_g[l], cos, sin)
        filt = _hyena_filter(L, hy_ffn_w1[l], hy_ffn_b1[l], hy_sin_freq[l], hy_ffn_w2[l],
                             hy_ffn_b2[l], hy_ffn_w3[l], hy_decay[l])
        o_hy = _hyena_branch(z_in, hy_conv_w[l], hy_conv_b[l], filt, hy_dskip[l])
        mixed = jnp.concatenate([_group_rmsnorm(o_att, attn_out_g[l]),
                                 _group_rmsnorm(o_hy, hy_out_g[l])], axis=-1)
        h = h + mixed @ w_out[l]
        hn = _rmsnorm(h, norm2_g[l])
        h = h + _peer(hn.reshape(B * L, D_MODEL), peer_wq[l], peer_keys[l], peer_u[l], peer_v[l]).reshape(B, L, D_MODEL)
    h = _rmsnorm(h, final_g)
    return h[:, N_META:]


def setup_inputs(seed: int = 0) -> dict:
    key = jax.random.key(seed)
    ks = jax.random.split(key, 32)
    f32 = jnp.float32
    nrm = lambda k, shp, sc: jax.random.normal(k, shp, f32) * sc
    gain = lambda k, shp: 1.0 + 0.02 * jax.random.normal(k, shp, f32)
    return {
        "x_prompt": nrm(ks[0], (BATCH, SEQ, D_MODEL), 1.0),
        "x_sample": nrm(ks[1], (DEC_BATCH, DEC_SEQ, D_MODEL), 1.0),
        "meta_tokens": nrm(ks[2], (N_META, D_MODEL), 1.0),
        "norm1_g": gain(ks[3], (DEPTH, D_MODEL)),
        "w_in": nrm(ks[4], (DEPTH, D_MODEL, IN_WIDTH), D_MODEL ** -0.5),
        "q_norm_g": gain(ks[5], (DEPTH, HEAD_DIM)),
        "k_norm_g": gain(ks[6], (DEPTH, HEAD_DIM)),
        "hy_conv_w": nrm(ks[7], (DEPTH, HY_SHORT, 3 * HYENA_WIDTH), HY_SHORT ** -0.5),
        "hy_conv_b": nrm(ks[8], (DEPTH, 3 * HYENA_WIDTH), 0.02),
        "hy_ffn_w1": nrm(ks[9], (DEPTH, HY_EMB, HY_FH), HY_EMB ** -0.5),
        "hy_ffn_b1": nrm(ks[10], (DEPTH, HY_FH), 0.02),
        "hy_sin_freq": gain(ks[11], (DEPTH, 2, HY_FH)),
        "hy_ffn_w2": nrm(ks[12], (DEPTH, HY_FH, HY_FH), HY_FH ** -0.5),
        "hy_ffn_b2": nrm(ks[13], (DEPTH, HY_FH), 0.02),
        "hy_ffn_w3": nrm(ks[14], (DEPTH, HY_FH, 2 * HYENA_WIDTH), 0.1 * HY_FH ** -0.5),
        "hy_decay": jax.random.uniform(ks[15], (DEPTH, 2, HYENA_WIDTH), f32, 3.0, 15.0),
        "hy_dskip": nrm(ks[16], (DEPTH, HYENA_WIDTH), 1.0),
        "attn_out_g": gain(ks[17], (DEPTH, N_HEADS, HEAD_DIM)),
        "hy_out_g": gain(ks[18], (DEPTH, HY_GROUPS, HY_GROUP_DIM)),
        "w_out": nrm(ks[19], (DEPTH, D_MODEL, D_MODEL), 0.5 * D_MODEL ** -0.5),
        "norm2_g": gain(ks[20], (DEPTH, D_MODEL)),
        "peer_wq": nrm(ks[21], (DEPTH, D_MODEL, PEER_HEADS * PEER_DKEY), D_MODEL ** -0.5),
        "peer_keys": nrm(ks[22], (DEPTH, 2, PEER_HEADS, PEER_NKEYS, PEER_DHALF), PEER_DHALF ** -0.5),
        "peer_u": nrm(ks[23], (DEPTH, PEER_EXPERTS, D_MODEL), D_MODEL ** -0.5),
        "peer_v": nrm(ks[24], (DEPTH, PEER_EXPERTS, D_MODEL), 0.5 * PEER_HEADS ** -0.5),
        "final_g": gain(ks[25], (D_MODEL,)),
    }


def reference(x_prompt, x_sample, meta_tokens, norm1_g, w_in, q_norm_g, k_norm_g, hy_conv_w, hy_conv_b,
              hy_ffn_w1, hy_ffn_b1, hy_sin_freq, hy_ffn_w2, hy_ffn_b2, hy_ffn_w3, hy_decay, hy_dskip,
              attn_out_g, hy_out_g, w_out, norm2_g, peer_wq, peer_keys, peer_u, peer_v, final_g):
    y_prompt = _encode(x_prompt, meta_tokens, norm1_g, w_in, q_norm_g, k_norm_g, hy_conv_w, hy_conv_b,
                       hy_ffn_w1, hy_ffn_b1, hy_sin_freq, hy_ffn_w2, hy_ffn_b2, hy_ffn_w3, hy_decay, hy_dskip,
                       attn_out_g, hy_out_g, w_out, norm2_g, peer_wq, peer_keys, peer_u, peer_v, final_g)
    y_sample = _encode(x_sample, meta_tokens, norm1_g, w_in, q_norm_g, k_norm_g, hy_conv_w, hy_conv_b,
                       hy_ffn_w1, hy_ffn_b1, hy_sin_freq, hy_ffn_w2, hy_ffn_b2, hy_ffn_w3, hy_decay, hy_dskip,
                       attn_out_g, hy_out_g, w_out, norm2_g, peer_wq, peer_keys, peer_u, peer_v, final_g)
    return (y_prompt, y_sample)
```

```python
import functools
import math

import jax
import jax.numpy as jnp
from jax import lax
from jax.experimental import pallas as pl
from jax.experimental.pallas import tpu as pltpu

N_META = 16
GRID_W = 64
HEAD_DIM = 64
ROPE_THETA = 10000.0
HY_BANDS = 16
PEER_TOPK = 16
NORM_EPS = 1e-6
LANE = 128
PADF = LANE - N_META
VMEM_LIMIT = 56 * 1024 * 1024

f32 = jnp.float32
bf16 = jnp.bfloat16


def _cparams(sem):
    return pltpu.CompilerParams(dimension_semantics=sem, vmem_limit_bytes=VMEM_LIMIT)


def _pick(n, cands):
    for c in cands:
        if n % c == 0:
            return c
    raise ValueError(f"no block size for {n}")


def _inproj_kernel(x_ref, g_ref, w_ref, bd_ref, c_ref, s_ref,
                   q_ref, k_ref, v_ref, z_ref, *, qw, kw, tr):
    r = pl.program_id(1)
    x = x_ref[...]
    ms = jnp.mean(x * x, axis=-1, keepdims=True)
    hn = (x * lax.rsqrt(ms + NORM_EPS) * g_ref[...]).astype(bf16)
    proj = jnp.dot(hn, w_ref[...], preferred_element_type=f32)
    qkw = qw + kw
    qk = proj[:, :qkw]
    qk_swapped = proj[:, qkw:2 * qkw]
    msq = jnp.dot((qk * qk).astype(bf16), bd_ref[...], preferred_element_type=f32) * (1.0 / HEAD_DIM)
    qkr = lax.rsqrt(msq + NORM_EPS) * (qk * c_ref[...] + qk_swapped * s_ref[...])
    q_ref[...] = qkr[:, :qw].astype(bf16)

    lane = lax.broadcasted_iota(jnp.int32, (tr, LANE), 1)
    lo = lane < HEAD_DIM

    def ext(a):
        ar = pltpu.roll(a, HEAD_DIM, axis=1)
        zero = jnp.zeros_like(a)
        return jnp.concatenate([jnp.where(lo, a, zero), jnp.where(lo, zero, ar),
                                jnp.where(lo, ar, zero), jnp.where(lo, zero, a)], axis=1)

    k_ref[...] = ext(qkr[:, qw:qkw]).astype(bf16)
    v_ref[...] = ext(proj[:, 2 * qkw:2 * qkw + kw]).astype(bf16)
    z = proj[:, 2 * qkw + kw:]
    pos = r * tr + lax.broadcasted_iota(jnp.int32, (tr, 1), 0)
    z_ref[...] = jnp.where(pos >= PADF, z, 0.0).astype(bf16)


def _inproj(h2d, g, w_ext, bd, ctab, stab, *, nb, lp, qw, kw, zw):
    d = h2d.shape[1]
    tr = _pick(lp, (384, 128))
    nr = lp // tr
    t = nb * lp
    wcols = w_ext.shape[1]
    qkw = qw + kw
    row = lambda b, r: (b * nr + r, 0)
    const = lambda b, r: (0, 0)
    return pl.pallas_call(
        functools.partial(_inproj_kernel, qw=qw, kw=kw, tr=tr),
        grid=(nb, nr),
        in_specs=[pl.BlockSpec((tr, d), row),
                  pl.BlockSpec((1, d), const),
                  pl.BlockSpec((d, wcols), const),
                  pl.BlockSpec((qkw, qkw), const),
                  pl.BlockSpec((tr, qkw), lambda b, r: (r, 0)),
                  pl.BlockSpec((tr, qkw), lambda b, r: (r, 0))],
        out_specs=[pl.BlockSpec((tr, qw), row),
                   pl.BlockSpec((tr, 4 * kw), row),
                   pl.BlockSpec((tr, 4 * kw), row),
                   pl.BlockSpec((tr, zw), row)],
        out_shape=[jax.ShapeDtypeStruct((t, qw), bf16),
                   jax.ShapeDtypeStruct((t, 4 * kw), bf16),
                   jax.ShapeDtypeStruct((t, 4 * kw), bf16),
                   jax.ShapeDtypeStruct((t, zw), bf16)],
        compiler_params=_cparams(("parallel", "parallel")),
        name="inproj",
    )(h2d, g, w_ext, bd, ctab, stab)


def _attn_kernel(q_ref, k_ref, v_ref, bias_ref, g_ref, o_ref, *, tq):
    lane = lax.broadcasted_iota(jnp.int32, (tq, LANE), 1)
    lo = lane < HEAD_DIM
    nt = (((1,), (1,)), ((), ()))
    for pr in range(q_ref.shape[1] // LANE):
        cols = slice(pr * LANE, (pr + 1) * LANE)
        qp = q_ref[:, cols]
        o = jnp.zeros((tq, LANE), f32)
        for hh in range(2):
            ext = slice(hh * LANE, (hh + 1) * LANE)
            s = lax.dot_general(qp, k_ref[:, ext], nt, preferred_element_type=f32) + bias_ref[...]
            m = jnp.max(s, axis=-1, keepdims=True)
            p = jnp.exp(s - m)
            l = jnp.sum(p, axis=-1, keepdims=True)
            oh = jnp.dot(p.astype(bf16), v_ref[:, ext], preferred_element_type=f32)
            o = o + oh * (1.0 / l)
        o2 = o * o
        s_all = jnp.sum(o2, axis=-1, keepdims=True)
        s_lo = jnp.sum(jnp.where(lo, o2, 0.0), axis=-1, keepdims=True)
        ms = jnp.where(lo, s_lo, s_all - s_lo) * (1.0 / HEAD_DIM)
        o_ref[:, cols] = (o * lax.rsqrt(ms + NORM_EPS) * g_ref[:, cols]).astype(bf16)


def _attention(q, kx, vx, bias, g_out, *, nb, lp, n_kv):
    t, qw = q.shape
    gw = qw // n_kv
    assert gw % LANE == 0
    tq = _pick(lp, (384, 128))
    nq = lp // tq
    return pl.pallas_call(
        functools.partial(_attn_kernel, tq=tq),
        grid=(nb, n_kv, nq),
        in_specs=[pl.BlockSpec((tq, gw), lambda b, h, i: (b * nq + i, h)),
                  pl.BlockSpec((lp, 2 * LANE), lambda b, h, i: (b, h)),
                  pl.BlockSpec((lp, 2 * LANE), lambda b, h, i: (b, h)),
                  pl.BlockSpec((1, lp), lambda b, h, i: (0, 0)),
                  pl.BlockSpec((1, gw), lambda b, h, i: (0, h))],
        out_specs=pl.BlockSpec((tq, gw), lambda b, h, i: (b * nq + i, h)),
        out_shape=jax.ShapeDtypeStruct((t, qw), bf16),
        compiler_params=_cparams(("parallel", "parallel", "parallel")),
        name="attention",
    )(q, kx, vx, bias, g_out)


def _hyfilter_kernel(bands_ref, w1_ref, b1_ref, sf_ref, w2_ref, b2_ref, w3_ref, dec_ref, o_ref,
                     *, tr, seq_l):
    r = pl.program_id(0)
    hp = lax.Precision.HIGHEST
    t = (r * tr + lax.broadcasted_iota(jnp.int32, (tr, 1), 0)).astype(f32)
    tn = t / (seq_l - 1)
    ang = ((2.0 * math.pi / seq_l) * t) * bands_ref[...]
    lane = lax.broadcasted_iota(jnp.int32, (tr, LANE), 1)
    feat = jnp.where(lane < HY_BANDS, jnp.cos(ang),
                     jnp.where(lane < 2 * HY_BANDS, jnp.sin(ang),
                               jnp.where(lane == 2 * HY_BANDS, tn, 0.0)))
    sf = sf_ref[...]
    h = jnp.sin(sf[0:1] * (jnp.dot(feat, w1_ref[...], precision=hp, preferred_element_type=f32) + b1_ref[...]))
    h = jnp.sin(sf[1:2] * (jnp.dot(h, w2_ref[...], precision=hp, preferred_element_type=f32) + b2_ref[...]))
    h = jnp.dot(h, w3_ref[...], precision=hp, preferred_element_type=f32)
    o_ref[...] = h * jnp.exp(-tn * dec_ref[...])


def _hyfilter(bands, w1p, b1, sf, w2, b2, w3, dec, *, lp, seq_l):
    tr = _pick(lp, (384, 128))
    fh = w2.shape[0]
    wout = w3.shape[1]
    const = lambda r: (0, 0)
    return pl.pallas_call(
        functools.partial(_hyfilter_kernel, tr=tr, seq_l=seq_l),
        grid=(lp // tr,),
        in_specs=[pl.BlockSpec((1, LANE), const), pl.BlockSpec((LANE, fh), const),
                  pl.BlockSpec((1, fh), const), pl.BlockSpec((2, fh), const),
                  pl.BlockSpec((fh, fh), const), pl.BlockSpec((1, fh), const),
                  pl.BlockSpec((fh, wout), const), pl.BlockSpec((1, wout), const)],
        out_specs=pl.BlockSpec((tr, wout), lambda r: (r, 0)),
        out_shape=jax.ShapeDtypeStruct((lp, wout), f32),
        compiler_params=_cparams(("parallel",)),
        name="hyena_filter",
    )(bands, w1p, b1, sf, w2, b2, w3, dec)


HALO = 16


def _hypre_kernel(z_ref, zp_ref, zn_ref, cw_ref, cb_ref, u_ref, x1_ref, *, tr, nr, hw):
    r = pl.program_id(1)
    z = z_ref[...].astype(f32)
    prev_row = jnp.where(r > 0, zp_ref[HALO - 1:HALO, :].astype(f32), 0.0)
    next_row = jnp.where(r < nr - 1, zn_ref[0:1, :].astype(f32), 0.0)
    ridx = lax.broadcasted_iota(jnp.int32, (tr, 1), 0)
    z_prev = jnp.where(ridx == 0, prev_row, pltpu.roll(z, 1, axis=0))
    z_next = jnp.where(ridx == tr - 1, next_row, pltpu.roll(z, tr - 1, axis=0))
    cw = cw_ref[...]
    zc = z_prev * cw[0:1] + z * cw[1:2] + z_next * cw[2:3] + cb_ref[...]
    x0 = zc[:, :hw]
    x1 = zc[:, hw:2 * hw]
    vv = zc[:, 2 * hw:]
    pos = r * tr + ridx
    u_ref[...] = jnp.where(pos >= PADF, vv * x0, 0.0)
    x1_ref[...] = x1


def _hypre(z, cw, cb, *, nb, lp, hw):
    t, zw = z.shape
    tr = _pick(lp, (384, 128))
    nr = lp // tr
    per = tr // HALO
    nh = t // HALO
    row = lambda b, r: (b * nr + r, 0)
    const = lambda b, r: (0, 0)
    return pl.pallas_call(
        functools.partial(_hypre_kernel, tr=tr, nr=nr, hw=hw),
        grid=(nb, nr),
        in_specs=[pl.BlockSpec((tr, zw), row),
                  pl.BlockSpec((HALO, zw), lambda b, r: (jnp.maximum((b * nr + r) * per - 1, 0), 0)),
                  pl.BlockSpec((HALO, zw), lambda b, r: (jnp.minimum((b * nr + r + 1) * per, nh - 1), 0)),
                  pl.BlockSpec((3, zw), const), pl.BlockSpec((1, zw), const)],
        out_specs=[pl.BlockSpec((tr, hw), row), pl.BlockSpec((tr, hw), row)],
        out_shape=[jax.ShapeDtypeStruct((t, hw), f32), jax.ShapeDtypeStruct((t, hw), f32)],
        compiler_params=_cparams(("parallel", "parallel")),
        name="hyena_shortconv",
    )(z, z, z, cw, cb)


def _hyconv_kernel(u_ref, h_ref, y_ref, rl_ref, *, nj, nb):
    jj = lax.broadcasted_iota(jnp.int32, (LANE, LANE), 0)
    ii = lax.broadcasted_iota(jnp.int32, (LANE, LANE), 1)
    upper = ii >= jj
    y_ref[...] = jnp.zeros(y_ref.shape, f32)

    def rolled(r):
        row = jnp.broadcast_to(h_ref[0, r:r + 1, :], (LANE, LANE))
        return pltpu.roll(row, 0, axis=1, stride=1, stride_axis=0)

    rl_ref[...] = rolled(0)
    for d in range(-(nj - 1), nj):
        hi = rolled(d + nj)
        tile = jnp.where(upper, hi, rl_ref[...]).astype(bf16)
        rl_ref[...] = hi
        j0, j1 = max(0, -d), min(nj, nj - d)
        src = u_ref[0, j0 * nb:j1 * nb, :].astype(bf16)
        dst = slice((j0 + d) * nb, (j1 + d) * nb)
        y_ref[0, dst, :] += jnp.dot(src, tile, preferred_element_type=f32)


def _hyconv(ut, hc, *, nj, nb):
    c, rows, _ = ut.shape
    return pl.pallas_call(
        functools.partial(_hyconv_kernel, nj=nj, nb=nb),
        grid=(c,),
        in_specs=[pl.BlockSpec((1, rows, LANE), lambda i: (i, 0, 0)),
                  pl.BlockSpec((1, 2 * nj, LANE), lambda i: (i, 0, 0))],
        out_specs=pl.BlockSpec((1, rows, LANE), lambda i: (i, 0, 0)),
        out_shape=jax.ShapeDtypeStruct((c, rows, LANE), f32),
        scratch_shapes=[pltpu.VMEM((LANE, LANE), f32)],
        compiler_params=_cparams(("parallel",)),
        name="hyena_longconv",
    )(ut, hc)


def _outproj_kernel(h_ref, oa_ref, y_ref, u_ref, x1_ref, ds_ref, gh_ref, bd_ref, w_ref, o_ref):
    ohy = (y_ref[...] + u_ref[...] * ds_ref[...]) * x1_ref[...]
    ms = jnp.dot((ohy * ohy).astype(bf16), bd_ref[...], preferred_element_type=f32) * (1.0 / HEAD_DIM)
    ohn = (ohy * lax.rsqrt(ms + NORM_EPS) * gh_ref[...]).astype(bf16)
    mixed = jnp.concatenate([oa_ref[...], ohn], axis=1)
    o_ref[...] = h_ref[...] + jnp.dot(mixed, w_ref[...], preferred_element_type=f32)


def _outproj(h2d, oa, y, u, x1, dskip, gh, bd, w_out):
    t, d = h2d.shape
    aw = oa.shape[1]
    hw = y.shape[1]
    tr = _pick(t, (384, 128))
    row = lambda r: (r, 0)
    const = lambda r: (0, 0)
    return pl.pallas_call(
        _outproj_kernel,
        grid=(t // tr,),
        in_specs=[pl.BlockSpec((tr, d), row), pl.BlockSpec((tr, aw), row),
                  pl.BlockSpec((tr, hw), row), pl.BlockSpec((tr, hw), row), pl.BlockSpec((tr, hw), row),
                  pl.BlockSpec((1, hw), const), pl.BlockSpec((1, hw), const),
                  pl.BlockSpec((hw, hw), const), pl.BlockSpec((d, d), const)],
        out_specs=pl.BlockSpec((tr, d), row),
        out_shape=jax.ShapeDtypeStruct((t, d), f32),
        compiler_params=_cparams(("parallel",)),
        name="outproj",
    )(h2d, oa, y, u, x1, dskip, gh, bd, w_out)


NTOP = PEER_TOPK + 1
TOPPAD = 24


def _gelu(x):
    return 0.5 * x * (1.0 + lax.erf(x * (1.0 / math.sqrt(2.0))))


def _peer_kernel(h_ref, g_ref, wq_ref, keys_ref, u_ref, vt_ref, o_ref,
                 hnt_ref, e1_ref, e2_ref, th_ref, work_ref, top_ref, cand_ref, acc_ref,
                 *, tb, eb, nheads, nkeys):
    e = pl.program_id(1)
    ne = pl.num_programs(1)
    neg = -jnp.inf

    @pl.when(e == 0)
    def _route():
        x = h_ref[...]
        ms = jnp.mean(x * x, axis=-1, keepdims=True)
        hn = x * lax.rsqrt(ms + NORM_EPS) * g_ref[...]
        hnt = hn.T.astype(bf16)
        hnt_ref[...] = hnt
        qt = jnp.dot(wq_ref[...], hnt, preferred_element_type=f32)
        dh = keys_ref.shape[2]
        for hp in range(2 * nheads):
            head, half = divmod(hp, 2)
            s = jnp.dot(keys_ref[hp], qt[hp * dh:(hp + 1) * dh, :].astype(bf16),
                        preferred_element_type=f32)
            work_ref[...] = s
            tops = []
            for _ in range(NTOP):
                w = work_ref[...]
                m = jnp.max(w, axis=0, keepdims=True)
                tops.append(m)
                work_ref[...] = jnp.where(w == m, neg, w)
            ex = jnp.exp(s - tops[0])
            if half == 0:
                e1_ref[head] = ex
            else:
                e2_ref[head] = ex
            for k in range(NTOP):
                top_ref[hp, k:k + 1, :] = jnp.exp(tops[k] - tops[0])
            top_ref[hp, NTOP:, :] = jnp.zeros((TOPPAD - NTOP, tb), f32)
        for head in range(nheads):
            b = top_ref[2 * head + 1]
            for k1 in range(NTOP):
                cand_ref[k1 * TOPPAD:(k1 + 1) * TOPPAD, :] = top_ref[2 * head, k1:k1 + 1, :] * b
            tops = []
            for _ in range(NTOP):
                w = cand_ref[...]
                m = jnp.max(w, axis=0, keepdims=True)
                tops.append(m)
                cand_ref[...] = jnp.where(w == m, -1.0, w)
            zsum = tops[0]
            for k in range(1, PEER_TOPK):
                zsum = zsum + jnp.maximum(tops[k], 0.0)
            inv = 1.0 / zsum
            e1_ref[head] = e1_ref[head] * inv
            mid = 0.5 * (jnp.maximum(tops[PEER_TOPK - 1], 0.0) + jnp.maximum(tops[PEER_TOPK], 0.0))
            th_ref[head:head + 1, :] = mid * inv
        acc_ref[...] = jnp.zeros(acc_ref.shape, f32)

    act = jnp.dot(u_ref[...], hnt_ref[...], preferred_element_type=f32)
    wa = []
    ipb = eb // nkeys
    i0 = pl.multiple_of(e * ipb, ipb)
    for ii in range(ipb):
        rows = slice(ii * nkeys, (ii + 1) * nkeys)
        chunks = []
        for c in range(tb // LANE):
            cols = slice(c * LANE, (c + 1) * LANE)
            w = jnp.zeros((nkeys, LANE), f32)
            for head in range(nheads):
                p = e1_ref[head, pl.ds(i0, ipb), cols][ii:ii + 1, :] * e2_ref[head, :, cols]
                w = w + jnp.where(p >= th_ref[head:head + 1, cols], p, 0.0)
            chunks.append((w * _gelu(act[rows, cols])).astype(bf16))
        wa.append(jnp.concatenate(chunks, axis=1))
    wa = jnp.concatenate(wa, axis=0)
    acc_ref[...] += jnp.dot(vt_ref[...], wa, preferred_element_type=f32)

    @pl.when(e == ne - 1)
    def _finish():
        o_ref[...] = h_ref[...] + acc_ref[...].T


def _peer(h2d, g, wq_t, keys, u_tab, vt_tab, *, nheads):
    t, d = h2d.shape
    nexp = u_tab.shape[0]
    nkeys = keys.shape[1]
    tb = _pick(t, (512, 384, 128))
    eb = 8 * nkeys
    qrows = wq_t.shape[0]
    tok = lambda i, e: (i, 0)
    const2 = lambda i, e: (0, 0)
    return pl.pallas_call(
        functools.partial(_peer_kernel, tb=tb, eb=eb, nheads=nheads, nkeys=nkeys),
        grid=(t // tb, nexp // eb),
        in_specs=[pl.BlockSpec((tb, d), tok),
                  pl.BlockSpec((1, d), const2),
                  pl.BlockSpec((qrows, d), const2),
                  pl.BlockSpec(keys.shape, lambda i, e: (0, 0, 0)),
                  pl.BlockSpec((eb, d), lambda i, e: (e, 0)),
                  pl.BlockSpec((d, eb), lambda i, e: (0, e))],
        out_specs=pl.BlockSpec((tb, d), tok),
        out_shape=jax.ShapeDtypeStruct((t, d), f32),
        scratch_shapes=[pltpu.VMEM((d, tb), bf16),
                        pltpu.VMEM((nheads, nkeys, tb), f32),
                        pltpu.VMEM((nheads, nkeys, tb), f32),
                        pltpu.VMEM((nheads, tb), f32),
                        pltpu.VMEM((nkeys, tb), f32),
                        pltpu.VMEM((2 * nheads, TOPPAD, tb), f32),
                        pltpu.VMEM((NTOP * TOPPAD, tb), f32),
                        pltpu.VMEM((d, tb), f32)],
        compiler_params=_cparams(("parallel", "arbitrary")),
        name="peer",
    )(h2d, g, wq_t, keys, u_tab, vt_tab)


def _final_kernel(h_ref, g_ref, o_ref):
    x = h_ref[0]
    ms = jnp.mean(x * x, axis=-1, keepdims=True)
    o_ref[0] = x * lax.rsqrt(ms + NORM_EPS) * g_ref[...]


def _final(h3d, g, *, b0, nbatch, s):
    _, lp, d = h3d.shape
    tr = LANE
    return pl.pallas_call(
        _final_kernel,
        grid=(nbatch, s // tr),
        in_specs=[pl.BlockSpec((1, tr, d), lambda b, r: (b0 + b, r + 1, 0)),
                  pl.BlockSpec((1, d), lambda b, r: (0, 0))],
        out_specs=pl.BlockSpec((1, tr, d), lambda b, r: (b, r, 0)),
        out_shape=jax.ShapeDtypeStruct((nbatch, s, d), f32),
        compiler_params=_cparams(("parallel", "parallel")),
        name="final_norm",
    )(h3d, g)


def _block_diag(n):
    a = jnp.arange(n) // HEAD_DIM
    return (a[:, None] == a[None, :]).astype(bf16)


def _rope_tables(s, lp, gq, gk, n_heads, n_kv):
    rows = s // GRID_W
    row = jnp.repeat(jnp.arange(rows, dtype=f32), GRID_W)
    col = jnp.tile(jnp.arange(GRID_W, dtype=f32), rows)
    per_axis = HEAD_DIM // 4
    inv = ROPE_THETA ** (-jnp.arange(per_axis, dtype=f32) / per_axis)
    ang = jnp.concatenate([row[:, None] * inv, col[:, None] * inv], axis=-1)
    ang = jnp.concatenate([jnp.zeros((lp - s, HEAD_DIM // 2), f32), ang], axis=0)
    cos, sin = jnp.cos(ang), jnp.sin(ang)
    cfull = jnp.repeat(cos, 2, axis=1)
    sfull = jnp.stack([-sin, sin], axis=-1).reshape(lp, HEAD_DIM)
    swap = jnp.arange(HEAD_DIM) ^ 1
    scale = HEAD_DIM ** -0.5
    cq, sq = cfull * gq * scale, sfull * gq[swap] * scale
    ck, sk = cfull * gk, sfull * gk[swap]
    ctab = jnp.concatenate([jnp.tile(cq, (1, n_heads)), jnp.tile(ck, (1, n_kv))], axis=1)
    stab = jnp.concatenate([jnp.tile(sq, (1, n_heads)), jnp.tile(sk, (1, n_kv))], axis=1)
    return ctab, stab


def kernel(x_prompt, x_sample, meta_tokens, norm1_g, w_in, q_norm_g, k_norm_g, hy_conv_w, hy_conv_b, hy_ffn_w1, hy_ffn_b1, hy_sin_freq, hy_ffn_w2, hy_ffn_b2, hy_ffn_w3, hy_decay, hy_dskip, attn_out_g, hy_out_g, w_out, norm2_g, peer_wq, peer_keys, peer_u, peer_v, final_g):
    depth, d, _ = w_in.shape
    b_p, s, _ = x_prompt.shape
    b_s = x_sample.shape[0]
    assert x_sample.shape[1] == s and s % LANE == 0 and s % GRID_W == 0
    nb = b_p + b_s
    seq_l = s + N_META
    lp = s + LANE
    nj = lp // LANE
    t = nb * lp
    n_heads = attn_out_g.shape[1]
    aw = n_heads * HEAD_DIM
    hw = d - aw
    kw = (w_in.shape[2] - aw - 3 * hw) // 2
    n_kv = kw // HEAD_DIM
    assert kw == LANE, "k/v lane-pair layout assumes two kv heads"
    peer_heads = peer_keys.shape[2]
    nkeys = peer_keys.shape[3]
    assert nkeys == LANE

    x = jnp.concatenate([x_prompt, x_sample], axis=0)
    front = jnp.concatenate([jnp.zeros((PADF, d), f32), meta_tokens.astype(f32)], axis=0)
    h = jnp.concatenate([jnp.broadcast_to(front[None], (nb, LANE, d)), x], axis=1).reshape(t, d)

    pos = jnp.arange(lp)
    bias = jnp.where(pos >= PADF, 0.0, -1e30).astype(f32)[None, :]
    bd_qk = _block_diag(aw + kw)
    bd_hy = _block_diag(hw)
    swap_q = jnp.arange(aw) ^ 1
    swap_k = jnp.arange(kw) ^ 1
    bands = jnp.linspace(1e-4, HY_BANDS - 1, HY_BANDS, dtype=f32)
    bands_p = jnp.zeros((1, LANE), f32).at[0, :HY_BANDS].set(bands).at[0, HY_BANDS:2 * HY_BANDS].set(bands)

    for l in range(depth):
        w = w_in[l]
        wq, wk = w[:, :aw], w[:, aw:aw + kw]
        wv, wz = w[:, aw + kw:aw + 2 * kw], w[:, aw + 2 * kw:]
        w_ext = jnp.concatenate([wq, wk, wq[:, swap_q], wk[:, swap_k], wv, wz], axis=1).astype(bf16)
        ctab, stab = _rope_tables(s, lp, q_norm_g[l], k_norm_g[l], n_heads, n_kv)
        q, kx, vx, z = _inproj(h, norm1_g[l][None, :], w_ext, bd_qk, ctab, stab,
                               nb=nb, lp=lp, qw=aw, kw=kw, zw=3 * hw)
        o_att = _attention(q, kx, vx, bias, attn_out_g[l].reshape(1, aw), nb=nb, lp=lp, n_kv=n_kv)

        w1 = hy_ffn_w1[l]
        fh = w1.shape[1]
        w1p = jnp.zeros((LANE, fh), f32).at[:2 * HY_BANDS].set(w1[1:]).at[2 * HY_BANDS].set(w1[0])
        filt = _hyfilter(bands_p, w1p, hy_ffn_b1[l][None], hy_sin_freq[l], hy_ffn_w2[l], hy_ffn_b2[l][None],
                         hy_ffn_w3[l], hy_decay[l].reshape(1, 2 * hw), lp=lp, seq_l=seq_l)
        hf, hb = filt[:seq_l, :hw], filt[:seq_l, hw:]
        lags = jnp.concatenate([jnp.zeros((lp - seq_l + 1, hw), f32), hb[:0:-1], hf,
                                jnp.zeros((lp - seq_l, hw), f32)], axis=0)
        hc = lags.T.reshape(hw, 2 * nj, LANE)

        u, x1 = _hypre(z, hy_conv_w[l], hy_conv_b[l][None], nb=nb, lp=lp, hw=hw)
        ut = u.reshape(nb, nj, LANE, hw).transpose(3, 1, 0, 2).reshape(hw, nj * nb, LANE)
        yt = _hyconv(ut, hc, nj=nj, nb=nb)
        y = yt.reshape(hw, nj, nb, LANE).transpose(2, 1, 3, 0).reshape(t, hw)

        h = _outproj(h, o_att, y, u, x1, hy_dskip[l][None], hy_out_g[l].reshape(1, hw), bd_hy,
                     w_out[l].astype(bf16))

        dh = peer_keys.shape[4]
        keys = peer_keys[l].transpose(1, 0, 2, 3).reshape(2 * peer_heads, nkeys, dh).astype(bf16)
        h = _peer(h, norm2_g[l][None], peer_wq[l].T.astype(bf16), keys,
                  peer_u[l].astype(bf16), peer_v[l].T.astype(bf16), nheads=peer_heads)

    h3 = h.reshape(nb, lp, d)
    y_prompt = _final(h3, final_g[None], b0=0, nbatch=b_p, s=s)
    y_sample = _final(h3, final_g[None], b0=b_p, nbatch=b_s, s=s)
    return (y_prompt, y_sample)
```

```python
import functools
import math

import jax
import jax.numpy as jnp
from jax import lax
from jax.experimental import pallas as pl
from jax.experimental.pallas import tpu as pltpu

N_META = 16
GRID_W = 64
HEAD_DIM = 64
ROPE_THETA = 10000.0
HY_BANDS = 16
PEER_TOPK = 16
NORM_EPS = 1e-6
LANE = 128
MXU_K = 256
PADF = LANE - N_META
VMEM_LIMIT = 56 * 1024 * 1024

f32 = jnp.float32
bf16 = jnp.bfloat16


def _cparams(sem):
    return pltpu.CompilerParams(dimension_semantics=sem, vmem_limit_bytes=VMEM_LIMIT)


def _pick(n, cands):
    for c in cands:
        if n % c == 0:
            return c
    raise ValueError(f"no block size for {n}")


def _inproj_kernel(x_ref, g_ref, w_ref, bd_ref, c_ref, s_ref,
                   q_ref, k_ref, v_ref, z_ref, *, qw, kw, tr):
    r = pl.program_id(1)
    x = x_ref[...]
    ms = jnp.mean(x * x, axis=-1, keepdims=True)
    hn = (x * lax.rsqrt(ms + NORM_EPS) * g_ref[...]).astype(bf16)
    proj = jnp.dot(hn, w_ref[...], preferred_element_type=f32)
    qkw = qw + kw
    qk = proj[:, :qkw]
    qk_swapped = proj[:, qkw:2 * qkw]
    msq = jnp.dot((qk * qk).astype(bf16), bd_ref[...], preferred_element_type=f32) * (1.0 / HEAD_DIM)
    qkr = lax.rsqrt(msq + NORM_EPS) * (qk * c_ref[...] + qk_swapped * s_ref[...])
    q_ref[...] = qkr[:, :qw].astype(bf16)

    lane = lax.broadcasted_iota(jnp.int32, (tr, LANE), 1)
    lo = lane < HEAD_DIM

    def ext(a):
        ar = pltpu.roll(a, HEAD_DIM, axis=1)
        zero = jnp.zeros_like(a)
        return jnp.concatenate([jnp.where(lo, a, zero), jnp.where(lo, zero, ar),
                                jnp.where(lo, ar, zero), jnp.where(lo, zero, a)], axis=1)

    k_ref[...] = ext(qkr[:, qw:qkw]).astype(bf16)
    v_ref[...] = ext(proj[:, 2 * qkw:2 * qkw + kw]).astype(bf16)
    z = proj[:, 2 * qkw + kw:]
    pos = r * tr + lax.broadcasted_iota(jnp.int32, (tr, 1), 0)
    z_ref[...] = jnp.where(pos >= PADF, z, 0.0).astype(bf16)


def _inproj(h2d, g, w_ext, bd, ctab, stab, *, nb, lp, qw, kw, zw):
    d = h2d.shape[1]
    tr = _pick(lp, (384, 128))
    nr = lp // tr
    t = nb * lp
    wcols = w_ext.shape[1]
    qkw = qw + kw
    row = lambda b, r: (b * nr + r, 0)
    const = lambda b, r: (0, 0)
    return pl.pallas_call(
        functools.partial(_inproj_kernel, qw=qw, kw=kw, tr=tr),
        grid=(nb, nr),
        in_specs=[pl.BlockSpec((tr, d), row),
                  pl.BlockSpec((1, d), const),
                  pl.BlockSpec((d, wcols), const),
                  pl.BlockSpec((qkw, qkw), const),
                  pl.BlockSpec((tr, qkw), lambda b, r: (r, 0)),
                  pl.BlockSpec((tr, qkw), lambda b, r: (r, 0))],
        out_specs=[pl.BlockSpec((tr, qw), row),
                   pl.BlockSpec((tr, 4 * kw), row),
                   pl.BlockSpec((tr, 4 * kw), row),
                   pl.BlockSpec((tr, zw), row)],
        out_shape=[jax.ShapeDtypeStruct((t, qw), bf16),
                   jax.ShapeDtypeStruct((t, 4 * kw), bf16),
                   jax.ShapeDtypeStruct((t, 4 * kw), bf16),
                   jax.ShapeDtypeStruct((t, zw), bf16)],
        compiler_params=_cparams(("parallel", "parallel")),
        name="inproj",
    )(h2d, g, w_ext, bd, ctab, stab)


def _attn_kernel(q_ref, k_ref, v_ref, g_ref, o_ref, *, tq):
    lane = lax.broadcasted_iota(jnp.int32, (tq, LANE), 1)
    lo = lane < HEAD_DIM
    real = lane >= PADF
    nt = (((1,), (1,)), ((), ()))
    for pr in range(q_ref.shape[1] // LANE):
        cols = slice(pr * LANE, (pr + 1) * LANE)
        qp = q_ref[:, cols]
        o = jnp.zeros((tq, LANE), f32)
        for hh in range(2):
            ext = slice(hh * LANE, (hh + 1) * LANE)
            s = lax.dot_general(qp, k_ref[:, ext], nt, preferred_element_type=f32)
            s0 = jnp.where(real, s[:, :LANE], -jnp.inf)
            s1 = s[:, LANE:]
            m = jnp.maximum(jnp.max(s0, axis=-1, keepdims=True), jnp.max(s1, axis=-1, keepdims=True))
            p0 = jnp.exp2(s0 - m)
            p1 = jnp.exp2(s1 - m)
            l = jnp.sum(p0, axis=-1, keepdims=True) + jnp.sum(p1, axis=-1, keepdims=True)
            oh = (jnp.dot(p0.astype(bf16), v_ref[:LANE, ext], preferred_element_type=f32)
                  + jnp.dot(p1.astype(bf16), v_ref[LANE:, ext], preferred_element_type=f32))
            o = o + oh * (1.0 / l)
        o2 = o * o
        s_all = jnp.sum(o2, axis=-1, keepdims=True)
        s_lo = jnp.sum(jnp.where(lo, o2, 0.0), axis=-1, keepdims=True)
        ms = jnp.where(lo, s_lo, s_all - s_lo) * (1.0 / HEAD_DIM)
        o_ref[:, cols] = (o * lax.rsqrt(ms + NORM_EPS) * g_ref[:, cols]).astype(bf16)


def _attention(q, kx, vx, g_out, *, nb, lp, n_kv):
    t, qw = q.shape
    gw = qw // n_kv
    assert gw % LANE == 0
    tq = _pick(lp, (384, 128))
    nq = lp // tq
    return pl.pallas_call(
        functools.partial(_attn_kernel, tq=tq),
        grid=(nb, n_kv, nq),
        in_specs=[pl.BlockSpec((tq, gw), lambda b, h, i: (b * nq + i, h)),
                  pl.BlockSpec((lp, 2 * LANE), lambda b, h, i: (b, h)),
                  pl.BlockSpec((lp, 2 * LANE), lambda b, h, i: (b, h)),
                  pl.BlockSpec((1, gw), lambda b, h, i: (0, h))],
        out_specs=pl.BlockSpec((tq, gw), lambda b, h, i: (b * nq + i, h)),
        out_shape=jax.ShapeDtypeStruct((t, qw), bf16),
        compiler_params=_cparams(("parallel", "parallel", "parallel")),
        name="attention",
    )(q, kx, vx, g_out)


def _hyfilter_kernel(bands_ref, w1_ref, b1_ref, sf_ref, w2_ref, b2_ref, w3_ref, dec_ref, o_ref,
                     *, tr, seq_l):
    r = pl.program_id(0)
    hp = lax.Precision.HIGHEST
    t = (r * tr + lax.broadcasted_iota(jnp.int32, (tr, 1), 0)).astype(f32)
    tn = t / (seq_l - 1)
    ang = ((2.0 * math.pi / seq_l) * t) * bands_ref[...]
    lane = lax.broadcasted_iota(jnp.int32, (tr, LANE), 1)
    feat = jnp.where(lane < HY_BANDS, jnp.cos(ang),
                     jnp.where(lane < 2 * HY_BANDS, jnp.sin(ang),
                               jnp.where(lane == 2 * HY_BANDS, tn, 0.0)))
    sf = sf_ref[...]
    h = jnp.sin(sf[0:1] * (jnp.dot(feat, w1_ref[...], precision=hp, preferred_element_type=f32) + b1_ref[...]))
    h = jnp.sin(sf[1:2] * (jnp.dot(h, w2_ref[...], precision=hp, preferred_element_type=f32) + b2_ref[...]))
    h = jnp.dot(h, w3_ref[...], precision=hp, preferred_element_type=f32)
    o_ref[...] = h * jnp.exp(-tn * dec_ref[...])


def _hyfilter(bands, w1p, b1, sf, w2, b2, w3, dec, *, lp, seq_l):
    tr = _pick(lp, (384, 128))
    fh = w2.shape[0]
    wout = w3.shape[1]
    const = lambda r: (0, 0)
    return pl.pallas_call(
        functools.partial(_hyfilter_kernel, tr=tr, seq_l=seq_l),
        grid=(lp // tr,),
        in_specs=[pl.BlockSpec((1, LANE), const), pl.BlockSpec((LANE, fh), const),
                  pl.BlockSpec((1, fh), const), pl.BlockSpec((2, fh), const),
                  pl.BlockSpec((fh, fh), const), pl.BlockSpec((1, fh), const),
                  pl.BlockSpec((fh, wout), const), pl.BlockSpec((1, wout), const)],
        out_specs=pl.BlockSpec((tr, wout), lambda r: (r, 0)),
        out_shape=jax.ShapeDtypeStruct((lp, wout), f32),
        compiler_params=_cparams(("parallel",)),
        name="hyena_filter",
    )(bands, w1p, b1, sf, w2, b2, w3, dec)


HALO = 16


def _hypre_kernel(z_ref, zp_ref, zn_ref, cw_ref, cb_ref, u_ref, x1_ref, *, tr, nr, hw):
    r = pl.program_id(1)
    z = z_ref[...].astype(f32)
    prev_row = jnp.where(r > 0, zp_ref[HALO - 1:HALO, :].astype(f32), 0.0)
    next_row = jnp.where(r < nr - 1, zn_ref[0:1, :].astype(f32), 0.0)
    ridx = lax.broadcasted_iota(jnp.int32, (tr, 1), 0)
    z_prev = jnp.where(ridx == 0, prev_row, pltpu.roll(z, 1, axis=0))
    z_next = jnp.where(ridx == tr - 1, next_row, pltpu.roll(z, tr - 1, axis=0))
    cw = cw_ref[...]
    zc = z_prev * cw[0:1] + z * cw[1:2] + z_next * cw[2:3] + cb_ref[...]
    x0 = zc[:, :hw]
    x1 = zc[:, hw:2 * hw]
    vv = zc[:, 2 * hw:]
    pos = r * tr + ridx
    u_ref[...] = jnp.where(pos >= PADF, vv * x0, 0.0)
    x1_ref[...] = x1


def _hypre(z, cw, cb, *, nb, lp, hw):
    t, zw = z.shape
    tr = _pick(lp, (384, 128))
    nr = lp // tr
    per = tr // HALO
    nh = t // HALO
    row = lambda b, r: (b * nr + r, 0)
    const = lambda b, r: (0, 0)
    return pl.pallas_call(
        functools.partial(_hypre_kernel, tr=tr, nr=nr, hw=hw),
        grid=(nb, nr),
        in_specs=[pl.BlockSpec((tr, zw), row),
                  pl.BlockSpec((HALO, zw), lambda b, r: (jnp.maximum((b * nr + r) * per - 1, 0), 0)),
                  pl.BlockSpec((HALO, zw), lambda b, r: (jnp.minimum((b * nr + r + 1) * per, nh - 1), 0)),
                  pl.BlockSpec((3, zw), const), pl.BlockSpec((1, zw), const)],
        out_specs=[pl.BlockSpec((tr, hw), row), pl.BlockSpec((tr, hw), row)],
        out_shape=[jax.ShapeDtypeStruct((t, hw), f32), jax.ShapeDtypeStruct((t, hw), f32)],
        compiler_params=_cparams(("parallel", "parallel")),
        name="hyena_shortconv",
    )(z, z, z, cw, cb)


def _hyconv_kernel(u_ref, h_ref, y_ref, rl_ref, *, nj, nb):
    jj = lax.broadcasted_iota(jnp.int32, (LANE, LANE), 0)
    ii = lax.broadcasted_iota(jnp.int32, (LANE, LANE), 1)
    upper = ii >= jj
    y_ref[...] = jnp.zeros(y_ref.shape, f32)

    def rolled(r):
        row = jnp.broadcast_to(h_ref[0, r:r + 1, :], (LANE, LANE))
        return pltpu.roll(row, 0, axis=1, stride=1, stride_axis=0)

    rl_ref[...] = rolled(0)
    for d in range(-(nj - 1), nj):
        hi = rolled(d + nj)
        tile = jnp.where(upper, hi, rl_ref[...]).astype(bf16)
        rl_ref[...] = hi
        j0, j1 = max(0, -d), min(nj, nj - d)
        src = u_ref[0, j0 * nb:j1 * nb, :].astype(bf16)
        dst = slice((j0 + d) * nb, (j1 + d) * nb)
        y_ref[0, dst, :] += jnp.dot(src, tile, preferred_element_type=f32)


def _hyconv(ut, hc, *, nj, nb):
    c, rows, _ = ut.shape
    return pl.pallas_call(
        functools.partial(_hyconv_kernel, nj=nj, nb=nb),
        grid=(c,),
        in_specs=[pl.BlockSpec((1, rows, LANE), lambda i: (i, 0, 0)),
                  pl.BlockSpec((1, 2 * nj, LANE), lambda i: (i, 0, 0))],
        out_specs=pl.BlockSpec((1, rows, LANE), lambda i: (i, 0, 0)),
        out_shape=jax.ShapeDtypeStruct((c, rows, LANE), f32),
        scratch_shapes=[pltpu.VMEM((LANE, LANE), f32)],
        compiler_params=_cparams(("parallel",)),
        name="hyena_longconv",
    )(ut, hc)


def _outproj_kernel(h_ref, oa_ref, y_ref, u_ref, x1_ref, ds_ref, gh_ref, bd_ref, w_ref, o_ref):
    ohy = (y_ref[...] + u_ref[...] * ds_ref[...]) * x1_ref[...]
    ms = jnp.dot((ohy * ohy).astype(bf16), bd_ref[...], preferred_element_type=f32) * (1.0 / HEAD_DIM)
    ohn = (ohy * lax.rsqrt(ms + NORM_EPS) * gh_ref[...]).astype(bf16)
    mixed = jnp.concatenate([oa_ref[...], ohn], axis=1)
    o_ref[...] = h_ref[...] + jnp.dot(mixed, w_ref[...], preferred_element_type=f32)


def _outproj(h2d, oa, y, u, x1, dskip, gh, bd, w_out):
    t, d = h2d.shape
    aw = oa.shape[1]
    hw = y.shape[1]
    tr = _pick(t, (384, 128))
    row = lambda r: (r, 0)
    const = lambda r: (0, 0)
    return pl.pallas_call(
        _outproj_kernel,
        grid=(t // tr,),
        in_specs=[pl.BlockSpec((tr, d), row), pl.BlockSpec((tr, aw), row),
                  pl.BlockSpec((tr, hw), row), pl.BlockSpec((tr, hw), row), pl.BlockSpec((tr, hw), row),
                  pl.BlockSpec((1, hw), const), pl.BlockSpec((1, hw), const),
                  pl.BlockSpec((hw, hw), const), pl.BlockSpec((d, d), const)],
        out_specs=pl.BlockSpec((tr, d), row),
        out_shape=jax.ShapeDtypeStruct((t, d), f32),
        compiler_params=_cparams(("parallel",)),
        name="outproj",
    )(h2d, oa, y, u, x1, dskip, gh, bd, w_out)


NTOP = PEER_TOPK + 1
TOPPAD = 24


def _gelu(x):
    return 0.5 * x * (1.0 + lax.erf(x * (1.0 / math.sqrt(2.0))))


CAND_SUB = 8
CAND_ROWS = TOPPAD + (CAND_SUB - 1) * CAND_SUB + 2 * CAND_SUB


def _peer_kernel(h_ref, g_ref, wq_ref, keys_ref, u_ref, vt_ref, o_ref,
                 hnt_ref, e1_ref, e2_ref, th_ref, work_ref, top_ref, cand_ref, acc_ref,
                 act_ref, wa_ref, *, tb, eb, nheads, nkeys, ne):
    e = pl.program_id(1)
    neg = -jnp.inf

    @pl.when(e == 0)
    def _route():
        x = h_ref[...]
        ms = jnp.mean(x * x, axis=-1, keepdims=True)
        hn = x * lax.rsqrt(ms + NORM_EPS) * g_ref[...]
        hnt = hn.T.astype(bf16)
        hnt_ref[...] = hnt
        qt = jnp.dot(wq_ref[...], hnt, preferred_element_type=f32)
        dh = keys_ref.shape[2]
        for hp in range(2 * nheads):
            head, half = divmod(hp, 2)
            s = jnp.dot(keys_ref[hp], qt[hp * dh:(hp + 1) * dh, :].astype(bf16),
                        preferred_element_type=f32)
            work_ref[...] = s
            for k in range(NTOP):
                w = work_ref[...]
                m = jnp.max(w, axis=0, keepdims=True)
                top_ref[hp, k:k + 1, :] = m
                work_ref[...] = jnp.where(w == m, neg, w)
            top_ref[hp, NTOP:, :] = jnp.full((TOPPAD - NTOP, tb), neg, f32)
            smax = top_ref[hp, 0:1, :]
            ex = jnp.exp(s - smax)
            if half == 0:
                e1_ref[head] = ex
            else:
                e2_ref[head] = ex
            top_ref[hp] = jnp.exp(top_ref[hp] - smax)
        for head in range(nheads):
            ta, tb_ = 2 * head, 2 * head + 1
            cand_ref[0:TOPPAD, :] = top_ref[ta, 0:1, :] * top_ref[tb_]
            for k1 in range(1, CAND_SUB):
                r0 = TOPPAD + (k1 - 1) * CAND_SUB
                cand_ref[r0:r0 + CAND_SUB, :] = top_ref[ta, k1:k1 + 1, :] * top_ref[tb_, 0:CAND_SUB, :]
            r0 = TOPPAD + (CAND_SUB - 1) * CAND_SUB
            cand_ref[r0:, :] = top_ref[ta, CAND_SUB:, :] * top_ref[tb_, 0:1, :]
            tops = []
            for _ in range(NTOP):
                w = cand_ref[...]
                m = jnp.max(w, axis=0, keepdims=True)
                tops.append(m)
                cand_ref[...] = jnp.where(w == m, -1.0, w)
            zsum = tops[0]
            for k in range(1, PEER_TOPK):
                zsum = zsum + jnp.maximum(tops[k], 0.0)
            inv = 1.0 / zsum
            e1_ref[head] = e1_ref[head] * inv
            mid = 0.5 * (jnp.maximum(tops[PEER_TOPK - 1], 0.0) + jnp.maximum(tops[PEER_TOPK], 0.0))
            th_ref[head:head + 1, :] = mid * inv
        acc_ref[...] = jnp.zeros(acc_ref.shape, f32)

    ipb = eb // nkeys
    i0 = pl.multiple_of(e * ipb, ipb)
    nchunk = tb // LANE
    npiece = eb // MXU_K
    rows_per_piece = MXU_K // nkeys

    def routing_weights(ii, c):
        cols = slice(c * LANE, (c + 1) * LANE)
        w = jnp.zeros((nkeys, LANE), f32)
        for head in range(nheads):
            p = e1_ref[head, pl.ds(i0, ipb), cols][ii:ii + 1, :] * e2_ref[head, :, cols]
            w = w + jnp.where(p >= th_ref[head:head + 1, cols], p, 0.0)
        return w

    act_ref[...] = jnp.dot(u_ref[...], hnt_ref[...], preferred_element_type=f32)
    for k in range(npiece):
        ks = slice(k * MXU_K, (k + 1) * MXU_K)
        for ii in range(k * rows_per_piece, (k + 1) * rows_per_piece):
            rows = slice(ii * nkeys, (ii + 1) * nkeys)
            for c in range(nchunk):
                cols = slice(c * LANE, (c + 1) * LANE)
                w = routing_weights(ii, c)
                wa_ref[rows, cols] = (w * _gelu(act_ref[rows, cols])).astype(bf16)
        acc_ref[...] += jnp.dot(vt_ref[:, ks], wa_ref[ks, :], preferred_element_type=f32)

    @pl.when(e == ne - 1)
    def _finish():
        o_ref[...] = h_ref[...] + acc_ref[...].T


def _peer(h2d, g, wq_t, keys, u_tab, vt_tab, *, nheads):
    t, d = h2d.shape
    nexp = u_tab.shape[0]
    nkeys = keys.shape[1]
    tb = _pick(t, (512, 384, 128))
    eb = 8 * nkeys
    qrows = wq_t.shape[0]
    ne = nexp // eb
    tok = lambda i, e: (i, 0)
    const2 = lambda i, e: (0, 0)
    return pl.pallas_call(
        functools.partial(_peer_kernel, tb=tb, eb=eb, nheads=nheads, nkeys=nkeys, ne=ne),
        grid=(t // tb, ne),
        in_specs=[pl.BlockSpec((tb, d), tok),
                  pl.BlockSpec((1, d), const2),
                  pl.BlockSpec((qrows, d), const2),
                  pl.BlockSpec(keys.shape, lambda i, e: (0, 0, 0)),
                  pl.BlockSpec((eb, d), lambda i, e: (e, 0)),
                  pl.BlockSpec((d, eb), lambda i, e: (0, e))],
        out_specs=pl.BlockSpec((tb, d), tok),
        out_shape=jax.ShapeDtypeStruct((t, d), f32),
        scratch_shapes=[pltpu.VMEM((d, tb), bf16),
                        pltpu.VMEM((nheads, nkeys, tb), f32),
                        pltpu.VMEM((nheads, nkeys, tb), f32),
                        pltpu.VMEM((nheads, tb), f32),
                        pltpu.VMEM((nkeys, tb), f32),
                        pltpu.VMEM((2 * nheads, TOPPAD, tb), f32),
                        pltpu.VMEM((CAND_ROWS, tb), f32),
                        pltpu.VMEM((d, tb), f32),
                        pltpu.VMEM((eb, tb), f32),
                        pltpu.VMEM((eb, tb), bf16)],
        compiler_params=_cparams(("parallel", "arbitrary")),
        name="peer",
    )(h2d, g, wq_t, keys, u_tab, vt_tab)


def _final_kernel(h_ref, g_ref, o_ref):
    x = h_ref[0]
    ms = jnp.mean(x * x, axis=-1, keepdims=True)
    o_ref[0] = x * lax.rsqrt(ms + NORM_EPS) * g_ref[...]


def _final(h3d, g, *, b0, nbatch, s):
    _, lp, d = h3d.shape
    tr = LANE
    return pl.pallas_call(
        _final_kernel,
        grid=(nbatch, s // tr),
        in_specs=[pl.BlockSpec((1, tr, d), lambda b, r: (b0 + b, r + 1, 0)),
                  pl.BlockSpec((1, d), lambda b, r: (0, 0))],
        out_specs=pl.BlockSpec((1, tr, d), lambda b, r: (b, r, 0)),
        out_shape=jax.ShapeDtypeStruct((nbatch, s, d), f32),
        compiler_params=_cparams(("parallel", "parallel")),
        name="final_norm",
    )(h3d, g)


def _block_diag(n):
    a = jnp.arange(n) // HEAD_DIM
    return (a[:, None] == a[None, :]).astype(bf16)


def _rope_tables(s, lp, gq, gk, n_heads, n_kv):
    rows = s // GRID_W
    row = jnp.repeat(jnp.arange(rows, dtype=f32), GRID_W)
    col = jnp.tile(jnp.arange(GRID_W, dtype=f32), rows)
    per_axis = HEAD_DIM // 4
    inv = ROPE_THETA ** (-jnp.arange(per_axis, dtype=f32) / per_axis)
    ang = jnp.concatenate([row[:, None] * inv, col[:, None] * inv], axis=-1)
    ang = jnp.concatenate([jnp.zeros((lp - s, HEAD_DIM // 2), f32), ang], axis=0)
    cos, sin = jnp.cos(ang), jnp.sin(ang)
    cfull = jnp.repeat(cos, 2, axis=1)
    sfull = jnp.stack([-sin, sin], axis=-1).reshape(lp, HEAD_DIM)
    swap = jnp.arange(HEAD_DIM) ^ 1
    scale = HEAD_DIM ** -0.5 * math.log2(math.e)
    cq, sq = cfull * gq * scale, sfull * gq[swap] * scale
    ck, sk = cfull * gk, sfull * gk[swap]
    ctab = jnp.concatenate([jnp.tile(cq, (1, n_heads)), jnp.tile(ck, (1, n_kv))], axis=1)
    stab = jnp.concatenate([jnp.tile(sq, (1, n_heads)), jnp.tile(sk, (1, n_kv))], axis=1)
    return ctab, stab


def kernel(x_prompt, x_sample, meta_tokens, norm1_g, w_in, q_norm_g, k_norm_g, hy_conv_w, hy_conv_b, hy_ffn_w1, hy_ffn_b1, hy_sin_freq, hy_ffn_w2, hy_ffn_b2, hy_ffn_w3, hy_decay, hy_dskip, attn_out_g, hy_out_g, w_out, norm2_g, peer_wq, peer_keys, peer_u, peer_v, final_g):
    depth, d, _ = w_in.shape
    b_p, s, _ = x_prompt.shape
    b_s = x_sample.shape[0]
    assert x_sample.shape[1] == s and s % LANE == 0 and s % GRID_W == 0
    nb = b_p + b_s
    seq_l = s + N_META
    lp = s + LANE
    nj = lp // LANE
    t = nb * lp
    n_heads = attn_out_g.shape[1]
    aw = n_heads * HEAD_DIM
    hw = d - aw
    kw = (w_in.shape[2] - aw - 3 * hw) // 2
    n_kv = kw // HEAD_DIM
    assert kw == LANE, "k/v lane-pair layout assumes two kv heads"
    peer_heads = peer_keys.shape[2]
    nkeys = peer_keys.shape[3]
    assert nkeys == LANE

    x = jnp.concatenate([x_prompt, x_sample], axis=0)
    front = jnp.concatenate([jnp.zeros((PADF, d), f32), meta_tokens.astype(f32)], axis=0)
    h = jnp.concatenate([jnp.broadcast_to(front[None], (nb, LANE, d)), x], axis=1).reshape(t, d)

    bd_qk = _block_diag(aw + kw)
    bd_hy = _block_diag(hw)
    swap_q = jnp.arange(aw) ^ 1
    swap_k = jnp.arange(kw) ^ 1
    bands = jnp.linspace(1e-4, HY_BANDS - 1, HY_BANDS, dtype=f32)
    bands_p = jnp.zeros((1, LANE), f32).at[0, :HY_BANDS].set(bands).at[0, HY_BANDS:2 * HY_BANDS].set(bands)

    for l in range(depth):
        w = w_in[l]
        wq, wk = w[:, :aw], w[:, aw:aw + kw]
        wv, wz = w[:, aw + kw:aw + 2 * kw], w[:, aw + 2 * kw:]
        w_ext = jnp.concatenate([wq, wk, wq[:, swap_q], wk[:, swap_k], wv, wz], axis=1).astype(bf16)
        ctab, stab = _rope_tables(s, lp, q_norm_g[l], k_norm_g[l], n_heads, n_kv)
        q, kx, vx, z = _inproj(h, norm1_g[l][None, :], w_ext, bd_qk, ctab, stab,
                               nb=nb, lp=lp, qw=aw, kw=kw, zw=3 * hw)
        o_att = _attention(q, kx, vx, attn_out_g[l].reshape(1, aw), nb=nb, lp=lp, n_kv=n_kv)

        w1 = hy_ffn_w1[l]
        fh = w1.shape[1]
        w1p = jnp.zeros((LANE, fh), f32).at[:2 * HY_BANDS].set(w1[1:]).at[2 * HY_BANDS].set(w1[0])
        filt = _hyfilter(bands_p, w1p, hy_ffn_b1[l][None], hy_sin_freq[l], hy_ffn_w2[l], hy_ffn_b2[l][None],
                         hy_ffn_w3[l], hy_decay[l].reshape(1, 2 * hw), lp=lp, seq_l=seq_l)
        hf, hb = filt[:seq_l, :hw], filt[:seq_l, hw:]
        lags = jnp.concatenate([jnp.zeros((lp - seq_l + 1, hw), f32), hb[:0:-1], hf,
                                jnp.zeros((lp - seq_l, hw), f32)], axis=0)
        hc = lags.T.reshape(hw, 2 * nj, LANE)

        u, x1 = _hypre(z, hy_conv_w[l], hy_conv_b[l][None], nb=nb, lp=lp, hw=hw)
        ut = u.reshape(nb, nj, LANE, hw).transpose(3, 1, 0, 2).reshape(hw, nj * nb, LANE)
        yt = _hyconv(ut, hc, nj=nj, nb=nb)
        y = yt.reshape(hw, nj, nb, LANE).transpose(2, 1, 3, 0).reshape(t, hw)

        h = _outproj(h, o_att, y, u, x1, hy_dskip[l][None], hy_out_g[l].reshape(1, hw), bd_hy,
                     w_out[l].astype(bf16))

        dh = peer_keys.shape[4]
        keys = peer_keys[l].transpose(1, 0, 2, 3).reshape(2 * peer_heads, nkeys, dh).astype(bf16)
        h = _peer(h, norm2_g[l][None], peer_wq[l].T.astype(bf16), keys,
                  peer_u[l].astype(bf16), peer_v[l].T.astype(bf16), nheads=peer_heads)

    h3 = h.reshape(nb, lp, d)
    y_prompt = _final(h3, final_g[None], b0=0, nbatch=b_p, s=s)
    y_sample = _final(h3, final_g[None], b0=b_p, nbatch=b_s, s=s)
    return (y_prompt, y_sample)
```

```python
import functools
import math

import jax
import jax.numpy as jnp
from jax import lax
from jax.experimental import pallas as pl
from jax.experimental.pallas import tpu as pltpu

N_META = 16
GRID_W = 64
HEAD_DIM = 64
ROPE_THETA = 10000.0
HY_BANDS = 16
PEER_TOPK = 16
NORM_EPS = 1e-6
LANE = 128
MXU_K = 256
PADF = LANE - N_META
VMEM_LIMIT = 56 * 1024 * 1024

f32 = jnp.float32
bf16 = jnp.bfloat16


def _cparams(sem):
    return pltpu.CompilerParams(dimension_semantics=sem, vmem_limit_bytes=VMEM_LIMIT)


def _pick(n, cands):
    for c in cands:
        if n % c == 0:
            return c
    raise ValueError(f"no block size for {n}")


def _inproj_kernel(x_ref, g_ref, w_ref, bd_ref, c_ref, s_ref,
                   q_ref, k_ref, v_ref, z_ref, *, qw, kw, tr):
    r = pl.program_id(1)
    x = x_ref[...]
    ms = jnp.mean(x * x, axis=-1, keepdims=True)
    hn = (x * lax.rsqrt(ms + NORM_EPS) * g_ref[...]).astype(bf16)
    proj = jnp.dot(hn, w_ref[...], preferred_element_type=f32)
    qkw = qw + kw
    qk = proj[:, :qkw]
    qk_swapped = proj[:, qkw:2 * qkw]
    msq = jnp.dot((qk * qk).astype(bf16), bd_ref[...], preferred_element_type=f32) * (1.0 / HEAD_DIM)
    qkr = lax.rsqrt(msq + NORM_EPS) * (qk * c_ref[...] + qk_swapped * s_ref[...])
    q_ref[...] = qkr[:, :qw].astype(bf16)

    lane = lax.broadcasted_iota(jnp.int32, (tr, LANE), 1)
    lo = lane < HEAD_DIM

    def ext(a):
        ar = pltpu.roll(a, HEAD_DIM, axis=1)
        zero = jnp.zeros_like(a)
        return jnp.concatenate([jnp.where(lo, a, zero), jnp.where(lo, zero, ar),
                                jnp.where(lo, ar, zero), jnp.where(lo, zero, a)], axis=1)

    k_ref[...] = ext(qkr[:, qw:qkw]).astype(bf16)
    v_ref[...] = ext(proj[:, 2 * qkw:2 * qkw + kw]).astype(bf16)
    z = proj[:, 2 * qkw + kw:]
    pos = r * tr + lax.broadcasted_iota(jnp.int32, (tr, 1), 0)
    z_ref[...] = jnp.where(pos >= PADF, z, 0.0).astype(bf16)


def _inproj(h2d, g, w_ext, bd, ctab, stab, *, nb, lp, qw, kw, zw):
    d = h2d.shape[1]
    tr = _pick(lp, (384, 128))
    nr = lp // tr
    t = nb * lp
    wcols = w_ext.shape[1]
    qkw = qw + kw
    row = lambda b, r: (b * nr + r, 0)
    const = lambda b, r: (0, 0)
    return pl.pallas_call(
        functools.partial(_inproj_kernel, qw=qw, kw=kw, tr=tr),
        grid=(nb, nr),
        in_specs=[pl.BlockSpec((tr, d), row),
                  pl.BlockSpec((1, d), const),
                  pl.BlockSpec((d, wcols), const),
                  pl.BlockSpec((qkw, qkw), const),
                  pl.BlockSpec((tr, qkw), lambda b, r: (r, 0)),
                  pl.BlockSpec((tr, qkw), lambda b, r: (r, 0))],
        out_specs=[pl.BlockSpec((tr, qw), row),
                   pl.BlockSpec((tr, 4 * kw), row),
                   pl.BlockSpec((tr, 4 * kw), row),
                   pl.BlockSpec((tr, zw), row)],
        out_shape=[jax.ShapeDtypeStruct((t, qw), bf16),
                   jax.ShapeDtypeStruct((t, 4 * kw), bf16),
                   jax.ShapeDtypeStruct((t, 4 * kw), bf16),
                   jax.ShapeDtypeStruct((t, zw), bf16)],
        compiler_params=_cparams(("parallel", "parallel")),
        name="inproj",
    )(h2d, g, w_ext, bd, ctab, stab)


SM_ROWS = 16


def _attn_kernel(q_ref, k_ref, v_ref, g_ref, o_ref, s_a, s_b, p_a, p_b, rl_ref, *, tq):
    lane = lax.broadcasted_iota(jnp.int32, (tq, LANE), 1)
    lo = lane < HEAD_DIM
    real = lax.broadcasted_iota(jnp.int32, (SM_ROWS, LANE), 1) >= PADF
    nt = (((1,), (1,)), ((), ()))
    heads = [(pr, hh) for pr in range(q_ref.shape[1] // LANE) for hh in range(2)]
    s_bufs, p_bufs = (s_a, s_b), (p_a, p_b)

    def scores(n):
        pr, hh = heads[n]
        s_bufs[n % 2][...] = lax.dot_general(q_ref[:, pr * LANE:(pr + 1) * LANE],
                                             k_ref[:, hh * LANE:(hh + 1) * LANE], nt,
                                             preferred_element_type=f32)

    def softmax_rows(s_ref, p_ref, r):
        rows = slice(r * SM_ROWS, (r + 1) * SM_ROWS)
        s0 = jnp.where(real, s_ref[rows, :LANE], -jnp.inf)
        s1 = s_ref[rows, LANE:]
        m = jnp.maximum(jnp.max(s0, axis=-1, keepdims=True), jnp.max(s1, axis=-1, keepdims=True))
        p0 = jnp.exp2(s0 - m)
        p1 = jnp.exp2(s1 - m)
        l = jnp.sum(p0, axis=-1, keepdims=True) + jnp.sum(p1, axis=-1, keepdims=True)
        p_ref[rows, :LANE] = p0.astype(bf16)
        p_ref[rows, LANE:] = p1.astype(bf16)
        rl_ref[rows, :] = jnp.broadcast_to(1.0 / l, (SM_ROWS, LANE))

    scores(0)
    for n, (pr, hh) in enumerate(heads):
        cols = slice(pr * LANE, (pr + 1) * LANE)
        if n + 1 < len(heads):
            scores(n + 1)
        for r in range(tq // SM_ROWS):
            softmax_rows(s_bufs[n % 2], p_bufs[n % 2], r)
        oh = jnp.dot(p_bufs[n % 2][...], v_ref[:, hh * LANE:(hh + 1) * LANE],
                     preferred_element_type=f32) * rl_ref[...]
        if hh == 0:
            o = oh
            continue
        o = o + oh
        o2 = o * o
        s_all = jnp.sum(o2, axis=-1, keepdims=True)
        s_lo = jnp.sum(jnp.where(lo, o2, 0.0), axis=-1, keepdims=True)
        ms = jnp.where(lo, s_lo, s_all - s_lo) * (1.0 / HEAD_DIM)
        o_ref[:, cols] = (o * lax.rsqrt(ms + NORM_EPS) * g_ref[:, cols]).astype(bf16)


def _attention(q, kx, vx, g_out, *, nb, lp, n_kv):
    t, qw = q.shape
    gw = qw // n_kv
    assert gw % LANE == 0
    tq = _pick(lp, (384, 128))
    nq = lp // tq
    return pl.pallas_call(
        functools.partial(_attn_kernel, tq=tq),
        grid=(nb, n_kv, nq),
        in_specs=[pl.BlockSpec((tq, gw), lambda b, h, i: (b * nq + i, h)),
                  pl.BlockSpec((lp, 2 * LANE), lambda b, h, i: (b, h)),
                  pl.BlockSpec((lp, 2 * LANE), lambda b, h, i: (b, h)),
                  pl.BlockSpec((1, gw), lambda b, h, i: (0, h))],
        out_specs=pl.BlockSpec((tq, gw), lambda b, h, i: (b * nq + i, h)),
        out_shape=jax.ShapeDtypeStruct((t, qw), bf16),
        scratch_shapes=[pltpu.VMEM((tq, lp), f32), pltpu.VMEM((tq, lp), f32),
                        pltpu.VMEM((tq, lp), bf16), pltpu.VMEM((tq, lp), bf16),
                        pltpu.VMEM((tq, LANE), f32)],
        compiler_params=_cparams(("parallel", "parallel", "parallel")),
        name="attention",
    )(q, kx, vx, g_out)


def _hyfilter_kernel(bands_ref, w1_ref, b1_ref, sf_ref, w2_ref, b2_ref, w3_ref, dec_ref, o_ref,
                     *, tr, seq_l):
    r = pl.program_id(0)
    hp = lax.Precision.HIGHEST
    t = (r * tr + lax.broadcasted_iota(jnp.int32, (tr, 1), 0)).astype(f32)
    tn = t / (seq_l - 1)
    ang = ((2.0 * math.pi / seq_l) * t) * bands_ref[...]
    lane = lax.broadcasted_iota(jnp.int32, (tr, LANE), 1)
    feat = jnp.where(lane < HY_BANDS, jnp.cos(ang),
                     jnp.where(lane < 2 * HY_BANDS, jnp.sin(ang),
                               jnp.where(lane == 2 * HY_BANDS, tn, 0.0)))
    sf = sf_ref[...]
    h = jnp.sin(sf[0:1] * (jnp.dot(feat, w1_ref[...], precision=hp, preferred_element_type=f32) + b1_ref[...]))
    h = jnp.sin(sf[1:2] * (jnp.dot(h, w2_ref[...], precision=hp, preferred_element_type=f32) + b2_ref[...]))
    h = jnp.dot(h, w3_ref[...], precision=hp, preferred_element_type=f32)
    o_ref[...] = h * jnp.exp(-tn * dec_ref[...])


def _hyfilter(bands, w1p, b1, sf, w2, b2, w3, dec, *, lp, seq_l):
    tr = _pick(lp, (384, 128))
    fh = w2.shape[0]
    wout = w3.shape[1]
    const = lambda r: (0, 0)
    return pl.pallas_call(
        functools.partial(_hyfilter_kernel, tr=tr, seq_l=seq_l),
        grid=(lp // tr,),
        in_specs=[pl.BlockSpec((1, LANE), const), pl.BlockSpec((LANE, fh), const),
                  pl.BlockSpec((1, fh), const), pl.BlockSpec((2, fh), const),
                  pl.BlockSpec((fh, fh), const), pl.BlockSpec((1, fh), const),
                  pl.BlockSpec((fh, wout), const), pl.BlockSpec((1, wout), const)],
        out_specs=pl.BlockSpec((tr, wout), lambda r: (r, 0)),
        out_shape=jax.ShapeDtypeStruct((lp, wout), f32),
        compiler_params=_cparams(("parallel",)),
        name="hyena_filter",
    )(bands, w1p, b1, sf, w2, b2, w3, dec)


HALO = 16


def _hypre_kernel(z_ref, zp_ref, zn_ref, cw_ref, cb_ref, u_ref, x1_ref, *, tr, nr, hw):
    r = pl.program_id(1)
    z = z_ref[...].astype(f32)
    prev_row = jnp.where(r > 0, zp_ref[HALO - 1:HALO, :].astype(f32), 0.0)
    next_row = jnp.where(r < nr - 1, zn_ref[0:1, :].astype(f32), 0.0)
    ridx = lax.broadcasted_iota(jnp.int32, (tr, 1), 0)
    z_prev = jnp.where(ridx == 0, prev_row, pltpu.roll(z, 1, axis=0))
    z_next = jnp.where(ridx == tr - 1, next_row, pltpu.roll(z, tr - 1, axis=0))
    cw = cw_ref[...]
    zc = z_prev * cw[0:1] + z * cw[1:2] + z_next * cw[2:3] + cb_ref[...]
    x0 = zc[:, :hw]
    x1 = zc[:, hw:2 * hw]
    vv = zc[:, 2 * hw:]
    pos = r * tr + ridx
    u_ref[...] = jnp.where(pos >= PADF, vv * x0, 0.0)
    x1_ref[...] = x1


def _hypre(z, cw, cb, *, nb, lp, hw):
    t, zw = z.shape
    tr = _pick(lp, (384, 128))
    nr = lp // tr
    per = tr // HALO
    nh = t // HALO
    row = lambda b, r: (b * nr + r, 0)
    const = lambda b, r: (0, 0)
    return pl.pallas_call(
        functools.partial(_hypre_kernel, tr=tr, nr=nr, hw=hw),
        grid=(nb, nr),
        in_specs=[pl.BlockSpec((tr, zw), row),
                  pl.BlockSpec((HALO, zw), lambda b, r: (jnp.maximum((b * nr + r) * per - 1, 0), 0)),
                  pl.BlockSpec((HALO, zw), lambda b, r: (jnp.minimum((b * nr + r + 1) * per, nh - 1), 0)),
                  pl.BlockSpec((3, zw), const), pl.BlockSpec((1, zw), const)],
        out_specs=[pl.BlockSpec((tr, hw), row), pl.BlockSpec((tr, hw), row)],
        out_shape=[jax.ShapeDtypeStruct((t, hw), f32), jax.ShapeDtypeStruct((t, hw), f32)],
        compiler_params=_cparams(("parallel", "parallel")),
        name="hyena_shortconv",
    )(z, z, z, cw, cb)


def _hyconv_kernel(u_ref, h_ref, y_ref, rl_ref, *, nj, nb):
    jj = lax.broadcasted_iota(jnp.int32, (LANE, LANE), 0)
    ii = lax.broadcasted_iota(jnp.int32, (LANE, LANE), 1)
    upper = ii >= jj
    y_ref[...] = jnp.zeros(y_ref.shape, f32)

    def rolled(r):
        row = jnp.broadcast_to(h_ref[0, r:r + 1, :], (LANE, LANE))
        return pltpu.roll(row, 0, axis=1, stride=1, stride_axis=0)

    rl_ref[...] = rolled(0)
    for d in range(-(nj - 1), nj):
        hi = rolled(d + nj)
        tile = jnp.where(upper, hi, rl_ref[...]).astype(bf16)
        rl_ref[...] = hi
        j0, j1 = max(0, -d), min(nj, nj - d)
        src = u_ref[0, j0 * nb:j1 * nb, :].astype(bf16)
        dst = slice((j0 + d) * nb, (j1 + d) * nb)
        y_ref[0, dst, :] += jnp.dot(src, tile, preferred_element_type=f32)


def _hyconv(ut, hc, *, nj, nb):
    c, rows, _ = ut.shape
    return pl.pallas_call(
        functools.partial(_hyconv_kernel, nj=nj, nb=nb),
        grid=(c,),
        in_specs=[pl.BlockSpec((1, rows, LANE), lambda i: (i, 0, 0)),
                  pl.BlockSpec((1, 2 * nj, LANE), lambda i: (i, 0, 0))],
        out_specs=pl.BlockSpec((1, rows, LANE), lambda i: (i, 0, 0)),
        out_shape=jax.ShapeDtypeStruct((c, rows, LANE), f32),
        scratch_shapes=[pltpu.VMEM((LANE, LANE), f32)],
        compiler_params=_cparams(("parallel",)),
        name="hyena_longconv",
    )(ut, hc)


def _outproj_kernel(h_ref, oa_ref, y_ref, u_ref, x1_ref, ds_ref, gh_ref, bd_ref, w_ref, o_ref):
    ohy = (y_ref[...] + u_ref[...] * ds_ref[...]) * x1_ref[...]
    ms = jnp.dot((ohy * ohy).astype(bf16), bd_ref[...], preferred_element_type=f32) * (1.0 / HEAD_DIM)
    ohn = (ohy * lax.rsqrt(ms + NORM_EPS) * gh_ref[...]).astype(bf16)
    mixed = jnp.concatenate([oa_ref[...], ohn], axis=1)
    o_ref[...] = h_ref[...] + jnp.dot(mixed, w_ref[...], preferred_element_type=f32)


def _outproj(h2d, oa, y, u, x1, dskip, gh, bd, w_out):
    t, d = h2d.shape
    aw = oa.shape[1]
    hw = y.shape[1]
    tr = _pick(t, (384, 128))
    row = lambda r: (r, 0)
    const = lambda r: (0, 0)
    return pl.pallas_call(
        _outproj_kernel,
        grid=(t // tr,),
        in_specs=[pl.BlockSpec((tr, d), row), pl.BlockSpec((tr, aw), row),
                  pl.BlockSpec((tr, hw), row), pl.BlockSpec((tr, hw), row), pl.BlockSpec((tr, hw), row),
                  pl.BlockSpec((1, hw), const), pl.BlockSpec((1, hw), const),
                  pl.BlockSpec((hw, hw), const), pl.BlockSpec((d, d), const)],
        out_specs=pl.BlockSpec((tr, d), row),
        out_shape=jax.ShapeDtypeStruct((t, d), f32),
        compiler_params=_cparams(("parallel",)),
        name="outproj",
    )(h2d, oa, y, u, x1, dskip, gh, bd, w_out)


NTOP = PEER_TOPK + 1
TOPPAD = 24


def _gelu(x):
    return 0.5 * x * (1.0 + lax.erf(x * (1.0 / math.sqrt(2.0))))


SUBLANES = 8


def _sort_network(n):
    def merge(lo, hi, r):
        step = r * 2
        if step < hi - lo:
            yield from merge(lo, hi, step)
            yield from merge(lo + r, hi, step)
            for i in range(lo + r, hi - r, step):
                yield (i, i + r)
        else:
            yield (lo, lo + r)

    def sort(lo, hi):
        if hi - lo >= 1:
            mid = lo + (hi - lo) // 2
            yield from sort(lo, mid)
            yield from sort(mid + 1, hi)
            yield from merge(lo, hi, 1)

    return list(sort(0, n - 1))


def _top_values(s, top_ref, hp, cols):
    nt = s.shape[0] // SUBLANES
    v = [s[i * SUBLANES:(i + 1) * SUBLANES, :] for i in range(nt)]
    for i, j in _sort_network(nt):
        v[i], v[j] = jnp.maximum(v[i], v[j]), jnp.minimum(v[i], v[j])
    v.append(jnp.full((SUBLANES, LANE), -jnp.inf, f32))
    for t in range(NTOP):
        m = jnp.max(v[0], axis=0, keepdims=True)
        top_ref[hp, t:t + 1, cols] = m
        pop = v[0] == m
        for k in range(min(nt, NTOP - 1 - t)):
            v[k] = jnp.where(pop, v[k + 1], v[k])


CAND_SUB = 8
CAND_ROWS = TOPPAD + (CAND_SUB - 1) * CAND_SUB + 2 * CAND_SUB


def _peer_kernel(h_ref, g_ref, wq_ref, keys_ref, u_ref, vt_ref, o_ref,
                 hnt_ref, e1_ref, e2_ref, th_ref, top_ref, cand_ref, acc_ref,
                 act_ref, wa_ref, *, tb, eb, nheads, nkeys, ne):
    e = pl.program_id(1)
    neg = -jnp.inf

    @pl.when(e == 0)
    def _route():
        x = h_ref[...]
        ms = jnp.mean(x * x, axis=-1, keepdims=True)
        hn = x * lax.rsqrt(ms + NORM_EPS) * g_ref[...]
        hnt = hn.T.astype(bf16)
        hnt_ref[...] = hnt
        qt = jnp.dot(wq_ref[...], hnt, preferred_element_type=f32)
        dh = keys_ref.shape[2]
        for hp in range(2 * nheads):
            head, half = divmod(hp, 2)
            s = jnp.dot(keys_ref[hp], qt[hp * dh:(hp + 1) * dh, :].astype(bf16),
                        preferred_element_type=f32)
            for c in range(tb // LANE):
                _top_values(s[:, c * LANE:(c + 1) * LANE], top_ref, hp, slice(c * LANE, (c + 1) * LANE))
            top_ref[hp, NTOP:, :] = jnp.full((TOPPAD - NTOP, tb), neg, f32)
            smax = top_ref[hp, 0:1, :]
            ex = jnp.exp(s - smax)
            if half == 0:
                e1_ref[head] = ex
            else:
                e2_ref[head] = ex
            top_ref[hp] = jnp.exp(top_ref[hp] - smax)
        for head in range(nheads):
            ta, tb_ = 2 * head, 2 * head + 1
            cand_ref[0:TOPPAD, :] = top_ref[ta, 0:1, :] * top_ref[tb_]
            for k1 in range(1, CAND_SUB):
                r0 = TOPPAD + (k1 - 1) * CAND_SUB
                cand_ref[r0:r0 + CAND_SUB, :] = top_ref[ta, k1:k1 + 1, :] * top_ref[tb_, 0:CAND_SUB, :]
            r0 = TOPPAD + (CAND_SUB - 1) * CAND_SUB
            cand_ref[r0:, :] = top_ref[ta, CAND_SUB:, :] * top_ref[tb_, 0:1, :]
            tops = []
            for _ in range(NTOP):
                w = cand_ref[...]
                m = jnp.max(w, axis=0, keepdims=True)
                tops.append(m)
                cand_ref[...] = jnp.where(w == m, -1.0, w)
            zsum = tops[0]
            for k in range(1, PEER_TOPK):
                zsum = zsum + jnp.maximum(tops[k], 0.0)
            inv = 1.0 / zsum
            e1_ref[head] = e1_ref[head] * inv
            mid = 0.5 * (jnp.maximum(tops[PEER_TOPK - 1], 0.0) + jnp.maximum(tops[PEER_TOPK], 0.0))
            th_ref[head:head + 1, :] = mid * inv
        acc_ref[...] = jnp.zeros(acc_ref.shape, f32)

    ipb = eb // nkeys
    i0 = pl.multiple_of(e * ipb, ipb)
    nchunk = tb // LANE
    npiece = eb // MXU_K
    rows_per_piece = MXU_K // nkeys

    def routing_weights(ii, c):
        cols = slice(c * LANE, (c + 1) * LANE)
        w = jnp.zeros((nkeys, LANE), f32)
        for head in range(nheads):
            p = e1_ref[head, pl.ds(i0, ipb), cols][ii:ii + 1, :] * e2_ref[head, :, cols]
            w = w + jnp.where(p >= th_ref[head:head + 1, cols], p, 0.0)
        return w

    act_ref[...] = jnp.dot(u_ref[...], hnt_ref[...], preferred_element_type=f32)
    for k in range(npiece):
        ks = slice(k * MXU_K, (k + 1) * MXU_K)
        for ii in range(k * rows_per_piece, (k + 1) * rows_per_piece):
            rows = slice(ii * nkeys, (ii + 1) * nkeys)
            for c in range(nchunk):
                cols = slice(c * LANE, (c + 1) * LANE)
                w = routing_weights(ii, c)
                wa_ref[rows, cols] = (w * _gelu(act_ref[rows, cols])).astype(bf16)
        acc_ref[...] += jnp.dot(vt_ref[:, ks], wa_ref[ks, :], preferred_element_type=f32)

    @pl.when(e == ne - 1)
    def _finish():
        o_ref[...] = h_ref[...] + acc_ref[...].T


def _peer(h2d, g, wq_t, keys, u_tab, vt_tab, *, nheads):
    t, d = h2d.shape
    nexp = u_tab.shape[0]
    nkeys = keys.shape[1]
    tb = _pick(t, (512, 384, 128))
    eb = 16 * nkeys
    qrows = wq_t.shape[0]
    ne = nexp // eb
    tok = lambda i, e: (i, 0)
    const2 = lambda i, e: (0, 0)
    return pl.pallas_call(
        functools.partial(_peer_kernel, tb=tb, eb=eb, nheads=nheads, nkeys=nkeys, ne=ne),
        grid=(t // tb, ne),
        in_specs=[pl.BlockSpec((tb, d), tok),
                  pl.BlockSpec((1, d), const2),
                  pl.BlockSpec((qrows, d), const2),
                  pl.BlockSpec(keys.shape, lambda i, e: (0, 0, 0)),
                  pl.BlockSpec((eb, d), lambda i, e: (e, 0)),
                  pl.BlockSpec((d, eb), lambda i, e: (0, e))],
        out_specs=pl.BlockSpec((tb, d), tok),
        out_shape=jax.ShapeDtypeStruct((t, d), f32),
        scratch_shapes=[pltpu.VMEM((d, tb), bf16),
                        pltpu.VMEM((nheads, nkeys, tb), f32),
                        pltpu.VMEM((nheads, nkeys, tb), f32),
                        pltpu.VMEM((nheads, tb), f32),
                        pltpu.VMEM((2 * nheads, TOPPAD, tb), f32),
                        pltpu.VMEM((CAND_ROWS, tb), f32),
                        pltpu.VMEM((d, tb), f32),
                        pltpu.VMEM((eb, tb), f32),
                        pltpu.VMEM((eb, tb), bf16)],
        compiler_params=_cparams(("parallel", "arbitrary")),
        name="peer",
    )(h2d, g, wq_t, keys, u_tab, vt_tab)


def _final_kernel(h_ref, g_ref, o_ref):
    x = h_ref[0]
    ms = jnp.mean(x * x, axis=-1, keepdims=True)
    o_ref[0] = x * lax.rsqrt(ms + NORM_EPS) * g_ref[...]


def _final(h3d, g, *, b0, nbatch, s):
    _, lp, d = h3d.shape
    tr = LANE
    return pl.pallas_call(
        _final_kernel,
        grid=(nbatch, s // tr),
        in_specs=[pl.BlockSpec((1, tr, d), lambda b, r: (b0 + b, r + 1, 0)),
                  pl.BlockSpec((1, d), lambda b, r: (0, 0))],
        out_specs=pl.BlockSpec((1, tr, d), lambda b, r: (b, r, 0)),
        out_shape=jax.ShapeDtypeStruct((nbatch, s, d), f32),
        compiler_params=_cparams(("parallel", "parallel")),
        name="final_norm",
    )(h3d, g)


def _block_diag(n):
    a = jnp.arange(n) // HEAD_DIM
    return (a[:, None] == a[None, :]).astype(bf16)


def _rope_tables(s, lp, gq, gk, n_heads, n_kv):
    rows = s // GRID_W
    row = jnp.repeat(jnp.arange(rows, dtype=f32), GRID_W)
    col = jnp.tile(jnp.arange(GRID_W, dtype=f32), rows)
    per_axis = HEAD_DIM // 4
    inv = ROPE_THETA ** (-jnp.arange(per_axis, dtype=f32) / per_axis)
    ang = jnp.concatenate([row[:, None] * inv, col[:, None] * inv], axis=-1)
    ang = jnp.concatenate([jnp.zeros((lp - s, HEAD_DIM // 2), f32), ang], axis=0)
    cos, sin = jnp.cos(ang), jnp.sin(ang)
    cfull = jnp.repeat(cos, 2, axis=1)
    sfull = jnp.stack([-sin, sin], axis=-1).reshape(lp, HEAD_DIM)
    swap = jnp.arange(HEAD_DIM) ^ 1
    scale = HEAD_DIM ** -0.5 * math.log2(math.e)
    cq, sq = cfull * gq * scale, sfull * gq[swap] * scale
    ck, sk = cfull * gk, sfull * gk[swap]
    ctab = jnp.concatenate([jnp.tile(cq, (1, n_heads)), jnp.tile(ck, (1, n_kv))], axis=1)
    stab = jnp.concatenate([jnp.tile(sq, (1, n_heads)), jnp.tile(sk, (1, n_kv))], axis=1)
    return ctab, stab


def kernel(x_prompt, x_sample, meta_tokens, norm1_g, w_in, q_norm_g, k_norm_g, hy_conv_w, hy_conv_b, hy_ffn_w1, hy_ffn_b1, hy_sin_freq, hy_ffn_w2, hy_ffn_b2, hy_ffn_w3, hy_decay, hy_dskip, attn_out_g, hy_out_g, w_out, norm2_g, peer_wq, peer_keys, peer_u, peer_v, final_g):
    depth, d, _ = w_in.shape
    b_p, s, _ = x_prompt.shape
    b_s = x_sample.shape[0]
    assert x_sample.shape[1] == s and s % LANE == 0 and s % GRID_W == 0
    nb = b_p + b_s
    seq_l = s + N_META
    lp = s + LANE
    nj = lp // LANE
    t = nb * lp
    n_heads = attn_out_g.shape[1]
    aw = n_heads * HEAD_DIM
    hw = d - aw
    kw = (w_in.shape[2] - aw - 3 * hw) // 2
    n_kv = kw // HEAD_DIM
    assert kw == LANE, "k/v lane-pair layout assumes two kv heads"
    peer_heads = peer_keys.shape[2]
    nkeys = peer_keys.shape[3]
    assert nkeys == LANE

    x = jnp.concatenate([x_prompt, x_sample], axis=0)
    front = jnp.concatenate([jnp.zeros((PADF, d), f32), meta_tokens.astype(f32)], axis=0)
    h = jnp.concatenate([jnp.broadcast_to(front[None], (nb, LANE, d)), x], axis=1).reshape(t, d)

    bd_qk = _block_diag(aw + kw)
    bd_hy = _block_diag(hw)
    swap_q = jnp.arange(aw) ^ 1
    swap_k = jnp.arange(kw) ^ 1
    bands = jnp.linspace(1e-4, HY_BANDS - 1, HY_BANDS, dtype=f32)
    bands_p = jnp.zeros((1, LANE), f32).at[0, :HY_BANDS].set(bands).at[0, HY_BANDS:2 * HY_BANDS].set(bands)

    for l in range(depth):
        w = w_in[l]
        wq, wk = w[:, :aw], w[:, aw:aw + kw]
        wv, wz = w[:, aw + kw:aw + 2 * kw], w[:, aw + 2 * kw:]
        w_ext = jnp.concatenate([wq, wk, wq[:, swap_q], wk[:, swap_k], wv, wz], axis=1).astype(bf16)
        ctab, stab = _rope_tables(s, lp, q_norm_g[l], k_norm_g[l], n_heads, n_kv)
        q, kx, vx, z = _inproj(h, norm1_g[l][None, :], w_ext, bd_qk, ctab, stab,
                               nb=nb, lp=lp, qw=aw, kw=kw, zw=3 * hw)
        o_att = _attention(q, kx, vx, attn_out_g[l].reshape(1, aw), nb=nb, lp=lp, n_kv=n_kv)

        w1 = hy_ffn_w1[l]
        fh = w1.shape[1]
        w1p = jnp.zeros((LANE, fh), f32).at[:2 * HY_BANDS].set(w1[1:]).at[2 * HY_BANDS].set(w1[0])
        filt = _hyfilter(bands_p, w1p, hy_ffn_b1[l][None], hy_sin_freq[l], hy_ffn_w2[l], hy_ffn_b2[l][None],
                         hy_ffn_w3[l], hy_decay[l].reshape(1, 2 * hw), lp=lp, seq_l=seq_l)
        hf, hb = filt[:seq_l, :hw], filt[:seq_l, hw:]
        lags = jnp.concatenate([jnp.zeros((lp - seq_l + 1, hw), f32), hb[:0:-1], hf,
                                jnp.zeros((lp - seq_l, hw), f32)], axis=0)
        hc = lags.T.reshape(hw, 2 * nj, LANE)

        u, x1 = _hypre(z, hy_conv_w[l], hy_conv_b[l][None], nb=nb, lp=lp, hw=hw)
        ut = u.reshape(nb, nj, LANE, hw).transpose(3, 1, 0, 2).reshape(hw, nj * nb, LANE)
        yt = _hyconv(ut, hc, nj=nj, nb=nb)
        y = yt.reshape(hw, nj, nb, LANE).transpose(2, 1, 3, 0).reshape(t, hw)

        h = _outproj(h, o_att, y, u, x1, hy_dskip[l][None], hy_out_g[l].reshape(1, hw), bd_hy,
                     w_out[l].astype(bf16))

        dh = peer_keys.shape[4]
        keys = peer_keys[l].transpose(1, 0, 2, 3).reshape(2 * peer_heads, nkeys, dh).astype(bf16)
        h = _peer(h, norm2_g[l][None], peer_wq[l].T.astype(bf16), keys,
                  peer_u[l].astype(bf16), peer_v[l].T.astype(bf16), nheads=peer_heads)

    h3 = h.reshape(nb, lp, d)
    y_prompt = _final(h3, final_g[None], b0=0, nbatch=b_p, s=s)
    y_sample = _final(h3, final_g[None], b0=b_p, nbatch=b_s, s=s)
    return (y_prompt, y_sample)
```

```python
import functools
import math

import jax
import jax.numpy as jnp
from jax import lax
from jax.experimental import pallas as pl
from jax.experimental.pallas import tpu as pltpu

N_META = 16
GRID_W = 64
HEAD_DIM = 64
ROPE_THETA = 10000.0
HY_BANDS = 16
PEER_TOPK = 16
NORM_EPS = 1e-6
LANE = 128
MXU_K = 256
PADF = LANE - N_META
VMEM_LIMIT = 56 * 1024 * 1024

f32 = jnp.float32
bf16 = jnp.bfloat16


def _cparams(sem):
    return pltpu.CompilerParams(dimension_semantics=sem, vmem_limit_bytes=VMEM_LIMIT)


def _pick(n, cands):
    for c in cands:
        if n % c == 0:
            return c
    raise ValueError(f"no block size for {n}")


def _inproj_kernel(x_ref, g_ref, w_ref, bd_ref, c_ref, s_ref,
                   q_ref, k_ref, v_ref, z_ref, *, qw, kw, tr):
    r = pl.program_id(1)
    x = x_ref[...]
    ms = jnp.mean(x * x, axis=-1, keepdims=True)
    hn = (x * lax.rsqrt(ms + NORM_EPS) * g_ref[...]).astype(bf16)
    proj = jnp.dot(hn, w_ref[...], preferred_element_type=f32)
    qkw = qw + kw
    qk = proj[:, :qkw]
    qk_swapped = proj[:, qkw:2 * qkw]
    msq = jnp.dot((qk * qk).astype(bf16), bd_ref[...], preferred_element_type=f32) * (1.0 / HEAD_DIM)
    qkr = lax.rsqrt(msq + NORM_EPS) * (qk * c_ref[...] + qk_swapped * s_ref[...])
    q_ref[...] = qkr[:, :qw].astype(bf16)

    lane = lax.broadcasted_iota(jnp.int32, (tr, LANE), 1)
    lo = lane < HEAD_DIM

    def ext(a):
        ar = pltpu.roll(a, HEAD_DIM, axis=1)
        zero = jnp.zeros_like(a)
        return jnp.concatenate([jnp.where(lo, a, zero), jnp.where(lo, zero, ar),
                                jnp.where(lo, ar, zero), jnp.where(lo, zero, a)], axis=1)

    k_ref[...] = ext(qkr[:, qw:qkw]).astype(bf16)
    v_ref[...] = ext(proj[:, 2 * qkw:2 * qkw + kw]).astype(bf16)
    z = proj[:, 2 * qkw + kw:]
    pos = r * tr + lax.broadcasted_iota(jnp.int32, (tr, 1), 0)
    z_ref[...] = jnp.where(pos >= PADF, z, 0.0).astype(bf16)


def _inproj(h2d, g, w_ext, bd, ctab, stab, *, nb, lp, qw, kw, zw):
    d = h2d.shape[1]
    tr = _pick(lp, (384, 128))
    nr = lp // tr
    t = nb * lp
    wcols = w_ext.shape[1]
    qkw = qw + kw
    row = lambda b, r: (b * nr + r, 0)
    const = lambda b, r: (0, 0)
    return pl.pallas_call(
        functools.partial(_inproj_kernel, qw=qw, kw=kw, tr=tr),
        grid=(nb, nr),
        in_specs=[pl.BlockSpec((tr, d), row),
                  pl.BlockSpec((1, d), const),
                  pl.BlockSpec((d, wcols), const),
                  pl.BlockSpec((qkw, qkw), const),
                  pl.BlockSpec((tr, qkw), lambda b, r: (r, 0)),
                  pl.BlockSpec((tr, qkw), lambda b, r: (r, 0))],
        out_specs=[pl.BlockSpec((tr, qw), row),
                   pl.BlockSpec((tr, 4 * kw), row),
                   pl.BlockSpec((tr, 4 * kw), row),
                   pl.BlockSpec((tr, zw), row)],
        out_shape=[jax.ShapeDtypeStruct((t, qw), bf16),
                   jax.ShapeDtypeStruct((t, 4 * kw), bf16),
                   jax.ShapeDtypeStruct((t, 4 * kw), bf16),
                   jax.ShapeDtypeStruct((t, zw), bf16)],
        compiler_params=_cparams(("parallel", "parallel")),
        name="inproj",
    )(h2d, g, w_ext, bd, ctab, stab)


SM_ROWS = 16


def _attn_kernel(q_ref, k_ref, v_ref, g_ref, o_ref, s_a, s_b, p_a, p_b, rl_ref, *, tq):
    lane = lax.broadcasted_iota(jnp.int32, (tq, LANE), 1)
    lo = lane < HEAD_DIM
    real = lax.broadcasted_iota(jnp.int32, (SM_ROWS, LANE), 1) >= PADF
    nt = (((1,), (1,)), ((), ()))
    heads = [(pr, hh) for pr in range(q_ref.shape[1] // LANE) for hh in range(2)]
    s_bufs, p_bufs = (s_a, s_b), (p_a, p_b)

    def scores(n):
        pr, hh = heads[n]
        s_bufs[n % 2][...] = lax.dot_general(q_ref[:, pr * LANE:(pr + 1) * LANE],
                                             k_ref[:, hh * LANE:(hh + 1) * LANE], nt,
                                             preferred_element_type=f32)

    def softmax_rows(s_ref, p_ref, r):
        rows = slice(r * SM_ROWS, (r + 1) * SM_ROWS)
        s0 = jnp.where(real, s_ref[rows, :LANE], -jnp.inf)
        s1 = s_ref[rows, LANE:]
        m = jnp.maximum(jnp.max(s0, axis=-1, keepdims=True), jnp.max(s1, axis=-1, keepdims=True))
        p0 = jnp.exp2(s0 - m)
        p1 = jnp.exp2(s1 - m)
        l = jnp.sum(p0, axis=-1, keepdims=True) + jnp.sum(p1, axis=-1, keepdims=True)
        p_ref[rows, :LANE] = p0.astype(bf16)
        p_ref[rows, LANE:] = p1.astype(bf16)
        rl_ref[rows, :] = jnp.broadcast_to(1.0 / l, (SM_ROWS, LANE))

    scores(0)
    for n, (pr, hh) in enumerate(heads):
        cols = slice(pr * LANE, (pr + 1) * LANE)
        if n + 1 < len(heads):
            scores(n + 1)
        for r in range(tq // SM_ROWS):
            softmax_rows(s_bufs[n % 2], p_bufs[n % 2], r)
        oh = jnp.dot(p_bufs[n % 2][...], v_ref[:, hh * LANE:(hh + 1) * LANE],
                     preferred_element_type=f32) * rl_ref[...]
        if hh == 0:
            o = oh
            continue
        o = o + oh
        o2 = o * o
        s_all = jnp.sum(o2, axis=-1, keepdims=True)
        s_lo = jnp.sum(jnp.where(lo, o2, 0.0), axis=-1, keepdims=True)
        ms = jnp.where(lo, s_lo, s_all - s_lo) * (1.0 / HEAD_DIM)
        o_ref[:, cols] = (o * lax.rsqrt(ms + NORM_EPS) * g_ref[:, cols]).astype(bf16)


def _attention(q, kx, vx, g_out, *, nb, lp, n_kv):
    t, qw = q.shape
    gw = qw // n_kv
    assert gw % LANE == 0
    tq = _pick(lp, (384, 128))
    nq = lp // tq
    return pl.pallas_call(
        functools.partial(_attn_kernel, tq=tq),
        grid=(nb, n_kv, nq),
        in_specs=[pl.BlockSpec((tq, gw), lambda b, h, i: (b * nq + i, h)),
                  pl.BlockSpec((lp, 2 * LANE), lambda b, h, i: (b, h)),
                  pl.BlockSpec((lp, 2 * LANE), lambda b, h, i: (b, h)),
                  pl.BlockSpec((1, gw), lambda b, h, i: (0, h))],
        out_specs=pl.BlockSpec((tq, gw), lambda b, h, i: (b * nq + i, h)),
        out_shape=jax.ShapeDtypeStruct((t, qw), bf16),
        scratch_shapes=[pltpu.VMEM((tq, lp), f32), pltpu.VMEM((tq, lp), f32),
                        pltpu.VMEM((tq, lp), bf16), pltpu.VMEM((tq, lp), bf16),
                        pltpu.VMEM((tq, LANE), f32)],
        compiler_params=_cparams(("parallel", "parallel", "parallel")),
        name="attention",
    )(q, kx, vx, g_out)


def _hyfilter_kernel(bands_ref, w1_ref, b1_ref, sf_ref, w2_ref, b2_ref, w3_ref, dec_ref, o_ref,
                     *, tr, seq_l):
    r = pl.program_id(0)
    hp = lax.Precision.HIGHEST
    t = (r * tr + lax.broadcasted_iota(jnp.int32, (tr, 1), 0)).astype(f32)
    tn = t / (seq_l - 1)
    ang = ((2.0 * math.pi / seq_l) * t) * bands_ref[...]
    lane = lax.broadcasted_iota(jnp.int32, (tr, LANE), 1)
    feat = jnp.where(lane < HY_BANDS, jnp.cos(ang),
                     jnp.where(lane < 2 * HY_BANDS, jnp.sin(ang),
                               jnp.where(lane == 2 * HY_BANDS, tn, 0.0)))
    sf = sf_ref[...]
    h = jnp.sin(sf[0:1] * (jnp.dot(feat, w1_ref[...], precision=hp, preferred_element_type=f32) + b1_ref[...]))
    h = jnp.sin(sf[1:2] * (jnp.dot(h, w2_ref[...], precision=hp, preferred_element_type=f32) + b2_ref[...]))
    h = jnp.dot(h, w3_ref[...], precision=hp, preferred_element_type=f32)
    o_ref[...] = h * jnp.exp(-tn * dec_ref[...])


def _hyfilter(bands, w1p, b1, sf, w2, b2, w3, dec, *, lp, seq_l):
    tr = _pick(lp, (384, 128))
    fh = w2.shape[0]
    wout = w3.shape[1]
    const = lambda r: (0, 0)
    return pl.pallas_call(
        functools.partial(_hyfilter_kernel, tr=tr, seq_l=seq_l),
        grid=(lp // tr,),
        in_specs=[pl.BlockSpec((1, LANE), const), pl.BlockSpec((LANE, fh), const),
                  pl.BlockSpec((1, fh), const), pl.BlockSpec((2, fh), const),
                  pl.BlockSpec((fh, fh), const), pl.BlockSpec((1, fh), const),
                  pl.BlockSpec((fh, wout), const), pl.BlockSpec((1, wout), const)],
        out_specs=pl.BlockSpec((tr, wout), lambda r: (r, 0)),
        out_shape=jax.ShapeDtypeStruct((lp, wout), f32),
        compiler_params=_cparams(("parallel",)),
        name="hyena_filter",
    )(bands, w1p, b1, sf, w2, b2, w3, dec)


HALO = 16


def _hypre_kernel(z_ref, zp_ref, zn_ref, cw_ref, cb_ref, u_ref, x1_ref, *, tr, nr, hw):
    r = pl.program_id(1)
    z = z_ref[...].astype(f32)
    prev_row = jnp.where(r > 0, zp_ref[HALO - 1:HALO, :].astype(f32), 0.0)
    next_row = jnp.where(r < nr - 1, zn_ref[0:1, :].astype(f32), 0.0)
    ridx = lax.broadcasted_iota(jnp.int32, (tr, 1), 0)
    z_prev = jnp.where(ridx == 0, prev_row, pltpu.roll(z, 1, axis=0))
    z_next = jnp.where(ridx == tr - 1, next_row, pltpu.roll(z, tr - 1, axis=0))
    cw = cw_ref[...]
    zc = z_prev * cw[0:1] + z * cw[1:2] + z_next * cw[2:3] + cb_ref[...]
    x0 = zc[:, :hw]
    x1 = zc[:, hw:2 * hw]
    vv = zc[:, 2 * hw:]
    pos = r * tr + ridx
    u_ref[...] = jnp.where(pos >= PADF, vv * x0, 0.0)
    x1_ref[...] = x1


def _hypre(z, cw, cb, *, nb, lp, hw):
    t, zw = z.shape
    tr = _pick(lp, (384, 128))
    nr = lp // tr
    per = tr // HALO
    nh = t // HALO
    row = lambda b, r: (b * nr + r, 0)
    const = lambda b, r: (0, 0)
    return pl.pallas_call(
        functools.partial(_hypre_kernel, tr=tr, nr=nr, hw=hw),
        grid=(nb, nr),
        in_specs=[pl.BlockSpec((tr, zw), row),
                  pl.BlockSpec((HALO, zw), lambda b, r: (jnp.maximum((b * nr + r) * per - 1, 0), 0)),
                  pl.BlockSpec((HALO, zw), lambda b, r: (jnp.minimum((b * nr + r + 1) * per, nh - 1), 0)),
                  pl.BlockSpec((3, zw), const), pl.BlockSpec((1, zw), const)],
        out_specs=[pl.BlockSpec((tr, hw), row), pl.BlockSpec((tr, hw), row)],
        out_shape=[jax.ShapeDtypeStruct((t, hw), f32), jax.ShapeDtypeStruct((t, hw), f32)],
        compiler_params=_cparams(("parallel", "parallel")),
        name="hyena_shortconv",
    )(z, z, z, cw, cb)


def _hyconv_kernel(u_ref, h_ref, y_ref, rl_ref, *, nj, nb):
    jj = lax.broadcasted_iota(jnp.int32, (LANE, LANE), 0)
    ii = lax.broadcasted_iota(jnp.int32, (LANE, LANE), 1)
    upper = ii >= jj
    y_ref[...] = jnp.zeros(y_ref.shape, f32)

    def rolled(r):
        row = jnp.broadcast_to(h_ref[0, r:r + 1, :], (LANE, LANE))
        return pltpu.roll(row, 0, axis=1, stride=1, stride_axis=0)

    rl_ref[...] = rolled(0)
    for d in range(-(nj - 1), nj):
        hi = rolled(d + nj)
        tile = jnp.where(upper, hi, rl_ref[...]).astype(bf16)
        rl_ref[...] = hi
        j0, j1 = max(0, -d), min(nj, nj - d)
        src = u_ref[0, j0 * nb:j1 * nb, :].astype(bf16)
        dst = slice((j0 + d) * nb, (j1 + d) * nb)
        y_ref[0, dst, :] += jnp.dot(src, tile, preferred_element_type=f32)


def _hyconv(ut, hc, *, nj, nb):
    c, rows, _ = ut.shape
    return pl.pallas_call(
        functools.partial(_hyconv_kernel, nj=nj, nb=nb),
        grid=(c,),
        in_specs=[pl.BlockSpec((1, rows, LANE), lambda i: (i, 0, 0)),
                  pl.BlockSpec((1, 2 * nj, LANE), lambda i: (i, 0, 0))],
        out_specs=pl.BlockSpec((1, rows, LANE), lambda i: (i, 0, 0)),
        out_shape=jax.ShapeDtypeStruct((c, rows, LANE), f32),
        scratch_shapes=[pltpu.VMEM((LANE, LANE), f32)],
        compiler_params=_cparams(("parallel",)),
        name="hyena_longconv",
    )(ut, hc)


def _outproj_kernel(h_ref, oa_ref, y_ref, u_ref, x1_ref, ds_ref, gh_ref, bd_ref, w_ref, o_ref):
    ohy = (y_ref[...] + u_ref[...] * ds_ref[...]) * x1_ref[...]
    ms = jnp.dot((ohy * ohy).astype(bf16), bd_ref[...], preferred_element_type=f32) * (1.0 / HEAD_DIM)
    ohn = (ohy * lax.rsqrt(ms + NORM_EPS) * gh_ref[...]).astype(bf16)
    mixed = jnp.concatenate([oa_ref[...], ohn], axis=1)
    o_ref[...] = h_ref[...] + jnp.dot(mixed, w_ref[...], preferred_element_type=f32)


def _outproj(h2d, oa, y, u, x1, dskip, gh, bd, w_out):
    t, d = h2d.shape
    aw = oa.shape[1]
    hw = y.shape[1]
    tr = _pick(t, (384, 128))
    row = lambda r: (r, 0)
    const = lambda r: (0, 0)
    return pl.pallas_call(
        _outproj_kernel,
        grid=(t // tr,),
        in_specs=[pl.BlockSpec((tr, d), row), pl.BlockSpec((tr, aw), row),
                  pl.BlockSpec((tr, hw), row), pl.BlockSpec((tr, hw), row), pl.BlockSpec((tr, hw), row),
                  pl.BlockSpec((1, hw), const), pl.BlockSpec((1, hw), const),
                  pl.BlockSpec((hw, hw), const), pl.BlockSpec((d, d), const)],
        out_specs=pl.BlockSpec((tr, d), row),
        out_shape=jax.ShapeDtypeStruct((t, d), f32),
        compiler_params=_cparams(("parallel",)),
        name="outproj",
    )(h2d, oa, y, u, x1, dskip, gh, bd, w_out)


NTOP = PEER_TOPK + 1
TOPPAD = 24


def _gelu_x2(x):
    return x * (1.0 + lax.erf(x * (1.0 / math.sqrt(2.0))))


SUBLANES = 8


def _sort_network(n):
    def merge(lo, hi, r):
        step = r * 2
        if step < hi - lo:
            yield from merge(lo, hi, step)
            yield from merge(lo + r, hi, step)
            for i in range(lo + r, hi - r, step):
                yield (i, i + r)
        else:
            yield (lo, lo + r)

    def sort(lo, hi):
        if hi - lo >= 1:
            mid = lo + (hi - lo) // 2
            yield from sort(lo, mid)
            yield from sort(mid + 1, hi)
            yield from merge(lo, hi, 1)

    return list(sort(0, n - 1))


def _top_values(s, top_ref, hp, cols):
    nt = s.shape[0] // SUBLANES
    v = [s[i * SUBLANES:(i + 1) * SUBLANES, :] for i in range(nt)]
    for i, j in _sort_network(nt):
        v[i], v[j] = jnp.maximum(v[i], v[j]), jnp.minimum(v[i], v[j])
    v.append(jnp.full((SUBLANES, LANE), -jnp.inf, f32))
    for t in range(NTOP):
        m = jnp.max(v[0], axis=0, keepdims=True)
        top_ref[hp, t:t + 1, cols] = m
        pop = v[0] == m
        for k in range(min(nt, NTOP - 1 - t)):
            v[k] = jnp.where(pop, v[k + 1], v[k])


CAND_SUB = 8
CAND_ROWS = TOPPAD + (CAND_SUB - 1) * CAND_SUB + 2 * CAND_SUB


def _peer_kernel(h_ref, g_ref, wq_ref, keys_ref, u_ref, vt_ref, o_ref,
                 hnt_ref, e1_ref, e2_ref, th_ref, top_ref, cand_ref, acc_ref,
                 act_ref, wa_ref, *, tb, eb, nheads, nkeys, ne):
    e = pl.program_id(1)
    neg = -jnp.inf

    @pl.when(e == 0)
    def _route():
        x = h_ref[...]
        ms = jnp.mean(x * x, axis=-1, keepdims=True)
        hn = x * lax.rsqrt(ms + NORM_EPS) * g_ref[...]
        hnt = hn.T.astype(bf16)
        hnt_ref[...] = hnt
        qt = jnp.dot(wq_ref[...], hnt, preferred_element_type=f32)
        dh = keys_ref.shape[2]
        for hp in range(2 * nheads):
            head, half = divmod(hp, 2)
            s = jnp.dot(keys_ref[hp], qt[hp * dh:(hp + 1) * dh, :].astype(bf16),
                        preferred_element_type=f32)
            for c in range(tb // LANE):
                _top_values(s[:, c * LANE:(c + 1) * LANE], top_ref, hp, slice(c * LANE, (c + 1) * LANE))
            top_ref[hp, NTOP:, :] = jnp.full((TOPPAD - NTOP, tb), neg, f32)
            smax = top_ref[hp, 0:1, :]
            ex = jnp.exp(s - smax)
            if half == 0:
                e1_ref[head] = ex
            else:
                e2_ref[head] = ex
            top_ref[hp] = jnp.exp(top_ref[hp] - smax)
        for head in range(nheads):
            ta, tb_ = 2 * head, 2 * head + 1
            cand_ref[0:TOPPAD, :] = top_ref[ta, 0:1, :] * top_ref[tb_]
            for k1 in range(1, CAND_SUB):
                r0 = TOPPAD + (k1 - 1) * CAND_SUB
                cand_ref[r0:r0 + CAND_SUB, :] = top_ref[ta, k1:k1 + 1, :] * top_ref[tb_, 0:CAND_SUB, :]
            r0 = TOPPAD + (CAND_SUB - 1) * CAND_SUB
            cand_ref[r0:, :] = top_ref[ta, CAND_SUB:, :] * top_ref[tb_, 0:1, :]
            tops = []
            for _ in range(NTOP):
                w = cand_ref[...]
                m = jnp.max(w, axis=0, keepdims=True)
                tops.append(m)
                cand_ref[...] = jnp.where(w == m, -1.0, w)
            zsum = tops[0]
            for k in range(1, PEER_TOPK):
                zsum = zsum + jnp.maximum(tops[k], 0.0)
            inv = 0.5 / zsum
            e1_ref[head] = e1_ref[head] * inv
            mid = 0.5 * (jnp.maximum(tops[PEER_TOPK - 1], 0.0) + jnp.maximum(tops[PEER_TOPK], 0.0))
            th_ref[head:head + 1, :] = mid * inv
        acc_ref[...] = jnp.zeros(acc_ref.shape, f32)

    ipb = eb // nkeys
    i0 = pl.multiple_of(e * ipb, ipb)
    nchunk = tb // LANE
    npiece = eb // MXU_K
    rows_per_piece = MXU_K // nkeys

    def routing_weights(ii, c):
        cols = slice(c * LANE, (c + 1) * LANE)
        w = None
        for head in range(nheads):
            p = e1_ref[head, pl.ds(i0, ipb), cols][ii:ii + 1, :] * e2_ref[head, :, cols]
            kept = jnp.where(p >= th_ref[head:head + 1, cols], p, 0.0)
            w = kept if w is None else w + kept
        return w

    act_ref[...] = jnp.dot(u_ref[...], hnt_ref[...], preferred_element_type=f32)
    for k in range(npiece):
        ks = slice(k * MXU_K, (k + 1) * MXU_K)
        for ii in range(k * rows_per_piece, (k + 1) * rows_per_piece):
            rows = slice(ii * nkeys, (ii + 1) * nkeys)
            for c in range(nchunk):
                cols = slice(c * LANE, (c + 1) * LANE)
                w = routing_weights(ii, c)
                wa_ref[rows, cols] = (w * _gelu_x2(act_ref[rows, cols])).astype(bf16)
        acc_ref[...] += jnp.dot(vt_ref[:, ks], wa_ref[ks, :], preferred_element_type=f32)

    @pl.when(e == ne - 1)
    def _finish():
        o_ref[...] = h_ref[...] + acc_ref[...].T


def _peer(h2d, g, wq_t, keys, u_tab, vt_tab, *, nheads):
    t, d = h2d.shape
    nexp = u_tab.shape[0]
    nkeys = keys.shape[1]
    tb = _pick(t, (512, 384, 128))
    eb = 16 * nkeys
    qrows = wq_t.shape[0]
    ne = nexp // eb
    tok = lambda i, e: (i, 0)
    const2 = lambda i, e: (0, 0)
    return pl.pallas_call(
        functools.partial(_peer_kernel, tb=tb, eb=eb, nheads=nheads, nkeys=nkeys, ne=ne),
        grid=(t // tb, ne),
        in_specs=[pl.BlockSpec((tb, d), tok),
                  pl.BlockSpec((1, d), const2),
                  pl.BlockSpec((qrows, d), const2),
                  pl.BlockSpec(keys.shape, lambda i, e: (0, 0, 0)),
                  pl.BlockSpec((eb, d), lambda i, e: (e, 0)),
                  pl.BlockSpec((d, eb), lambda i, e: (0, e))],
        out_specs=pl.BlockSpec((tb, d), tok),
        out_shape=jax.ShapeDtypeStruct((t, d), f32),
        scratch_shapes=[pltpu.VMEM((d, tb), bf16),
                        pltpu.VMEM((nheads, nkeys, tb), f32),
                        pltpu.VMEM((nheads, nkeys, tb), f32),
                        pltpu.VMEM((nheads, tb), f32),
                        pltpu.VMEM((2 * nheads, TOPPAD, tb), f32),
                        pltpu.VMEM((CAND_ROWS, tb), f32),
                        pltpu.VMEM((d, tb), f32),
                        pltpu.VMEM((eb, tb), f32),
                        pltpu.VMEM((eb, tb), bf16)],
        compiler_params=_cparams(("parallel", "arbitrary")),
        name="peer",
    )(h2d, g, wq_t, keys, u_tab, vt_tab)


def _final_kernel(h_ref, g_ref, o_ref):
    x = h_ref[0]
    ms = jnp.mean(x * x, axis=-1, keepdims=True)
    o_ref[0] = x * lax.rsqrt(ms + NORM_EPS) * g_ref[...]


def _final(h3d, g, *, b0, nbatch, s):
    _, lp, d = h3d.shape
    tr = LANE
    return pl.pallas_call(
        _final_kernel,
        grid=(nbatch, s // tr),
        in_specs=[pl.BlockSpec((1, tr, d), lambda b, r: (b0 + b, r + 1, 0)),
                  pl.BlockSpec((1, d), lambda b, r: (0, 0))],
        out_specs=pl.BlockSpec((1, tr, d), lambda b, r: (b, r, 0)),
        out_shape=jax.ShapeDtypeStruct((nbatch, s, d), f32),
        compiler_params=_cparams(("parallel", "parallel")),
        name="final_norm",
    )(h3d, g)


def _block_diag(n):
    a = jnp.arange(n) // HEAD_DIM
    return (a[:, None] == a[None, :]).astype(bf16)


def _rope_tables(s, lp, gq, gk, n_heads, n_kv):
    rows = s // GRID_W
    row = jnp.repeat(jnp.arange(rows, dtype=f32), GRID_W)
    col = jnp.tile(jnp.arange(GRID_W, dtype=f32), rows)
    per_axis = HEAD_DIM // 4
    inv = ROPE_THETA ** (-jnp.arange(per_axis, dtype=f32) / per_axis)
    ang = jnp.concatenate([row[:, None] * inv, col[:, None] * inv], axis=-1)
    ang = jnp.concatenate([jnp.zeros((lp - s, HEAD_DIM // 2), f32), ang], axis=0)
    cos, sin = jnp.cos(ang), jnp.sin(ang)
    cfull = jnp.repeat(cos, 2, axis=1)
    sfull = jnp.stack([-sin, sin], axis=-1).reshape(lp, HEAD_DIM)
    swap = jnp.arange(HEAD_DIM) ^ 1
    scale = HEAD_DIM ** -0.5 * math.log2(math.e)
    cq, sq = cfull * gq * scale, sfull * gq[swap] * scale
    ck, sk = cfull * gk, sfull * gk[swap]
    ctab = jnp.concatenate([jnp.tile(cq, (1, n_heads)), jnp.tile(ck, (1, n_kv))], axis=1)
    stab = jnp.concatenate([jnp.tile(sq, (1, n_heads)), jnp.tile(sk, (1, n_kv))], axis=1)
    return ctab, stab


def kernel(x_prompt, x_sample, meta_tokens, norm1_g, w_in, q_norm_g, k_norm_g, hy_conv_w, hy_conv_b, hy_ffn_w1, hy_ffn_b1, hy_sin_freq, hy_ffn_w2, hy_ffn_b2, hy_ffn_w3, hy_decay, hy_dskip, attn_out_g, hy_out_g, w_out, norm2_g, peer_wq, peer_keys, peer_u, peer_v, final_g):
    depth, d, _ = w_in.shape
    b_p, s, _ = x_prompt.shape
    b_s = x_sample.shape[0]
    assert x_sample.shape[1] == s and s % LANE == 0 and s % GRID_W == 0
    nb = b_p + b_s
    seq_l = s + N_META
    lp = s + LANE
    nj = lp // LANE
    t = nb * lp
    n_heads = attn_out_g.shape[1]
    aw = n_heads * HEAD_DIM
    hw = d - aw
    kw = (w_in.shape[2] - aw - 3 * hw) // 2
    n_kv = kw // HEAD_DIM
    assert kw == LANE, "k/v lane-pair layout assumes two kv heads"
    peer_heads = peer_keys.shape[2]
    nkeys = peer_keys.shape[3]
    assert nkeys == LANE

    x = jnp.concatenate([x_prompt, x_sample], axis=0)
    front = jnp.concatenate([jnp.zeros((PADF, d), f32), meta_tokens.astype(f32)], axis=0)
    h = jnp.concatenate([jnp.broadcast_to(front[None], (nb, LANE, d)), x], axis=1).reshape(t, d)

    bd_qk = _block_diag(aw + kw)
    bd_hy = _block_diag(hw)
    swap_q = jnp.arange(aw) ^ 1
    swap_k = jnp.arange(kw) ^ 1
    bands = jnp.linspace(1e-4, HY_BANDS - 1, HY_BANDS, dtype=f32)
    bands_p = jnp.zeros((1, LANE), f32).at[0, :HY_BANDS].set(bands).at[0, HY_BANDS:2 * HY_BANDS].set(bands)

    for l in range(depth):
        w = w_in[l]
        wq, wk = w[:, :aw], w[:, aw:aw + kw]
        wv, wz = w[:, aw + kw:aw + 2 * kw], w[:, aw + 2 * kw:]
        w_ext = jnp.concatenate([wq, wk, wq[:, swap_q], wk[:, swap_k], wv, wz], axis=1).astype(bf16)
        ctab, stab = _rope_tables(s, lp, q_norm_g[l], k_norm_g[l], n_heads, n_kv)
        q, kx, vx, z = _inproj(h, norm1_g[l][None, :], w_ext, bd_qk, ctab, stab,
                               nb=nb, lp=lp, qw=aw, kw=kw, zw=3 * hw)
        o_att = _attention(q, kx, vx, attn_out_g[l].reshape(1, aw), nb=nb, lp=lp, n_kv=n_kv)

        w1 = hy_ffn_w1[l]
        fh = w1.shape[1]
        w1p = jnp.zeros((LANE, fh), f32).at[:2 * HY_BANDS].set(w1[1:]).at[2 * HY_BANDS].set(w1[0])
        filt = _hyfilter(bands_p, w1p, hy_ffn_b1[l][None], hy_sin_freq[l], hy_ffn_w2[l], hy_ffn_b2[l][None],
                         hy_ffn_w3[l], hy_decay[l].reshape(1, 2 * hw), lp=lp, seq_l=seq_l)
        hf, hb = filt[:seq_l, :hw], filt[:seq_l, hw:]
        lags = jnp.concatenate([jnp.zeros((lp - seq_l + 1, hw), f32), hb[:0:-1], hf,
                                jnp.zeros((lp - seq_l, hw), f32)], axis=0)
        hc = lags.T.reshape(hw, 2 * nj, LANE)

        u, x1 = _hypre(z, hy_conv_w[l], hy_conv_b[l][None], nb=nb, lp=lp, hw=hw)
        ut = u.reshape(nb, nj, LANE, hw).transpose(3, 1, 0, 2).reshape(hw, nj * nb, LANE)
        yt = _hyconv(ut, hc, nj=nj, nb=nb)
        y = yt.reshape(hw, nj, nb, LANE).transpose(2, 1, 3, 0).reshape(t, hw)

        h = _outproj(h, o_att, y, u, x1, hy_dskip[l][None], hy_out_g[l].reshape(1, hw), bd_hy,
                     w_out[l].astype(bf16))

        dh = peer_keys.shape[4]
        keys = peer_keys[l].transpose(1, 0, 2, 3).reshape(2 * peer_heads, nkeys, dh).astype(bf16)
        h = _peer(h, norm2_g[l][None], peer_wq[l].T.astype(bf16), keys,
                  peer_u[l].astype(bf16), peer_v[l].T.astype(bf16), nheads=peer_heads)

    h3 = h.reshape(nb, lp, d)
    y_prompt = _final(h3, final_g[None], b0=0, nbatch=b_p, s=s)
    y_sample = _final(h3, final_g[None], b0=b_p, nbatch=b_s, s=s)
    return (y_prompt, y_sample)
```

```python
import functools
import math

import jax
import jax.numpy as jnp
from jax import lax
from jax.experimental import pallas as pl
from jax.experimental.pallas import tpu as pltpu

N_META = 16
GRID_W = 64
HEAD_DIM = 64
ROPE_THETA = 10000.0
HY_BANDS = 16
PEER_TOPK = 16
NORM_EPS = 1e-6
LANE = 128
MXU_K = 256
PADF = LANE - N_META
VMEM_LIMIT = 56 * 1024 * 1024

f32 = jnp.float32
bf16 = jnp.bfloat16


def _cparams(sem):
    return pltpu.CompilerParams(dimension_semantics=sem, vmem_limit_bytes=VMEM_LIMIT)


def _pick(n, cands):
    for c in cands:
        if n % c == 0:
            return c
    raise ValueError(f"no block size for {n}")


def _inproj_kernel(x_ref, g_ref, w_ref, bd_ref, c_ref, s_ref,
                   q_ref, k_ref, v_ref, z_ref, *, qw, kw, tr):
    r = pl.program_id(1)
    x = x_ref[...]
    ms = jnp.mean(x * x, axis=-1, keepdims=True)
    hn = (x * lax.rsqrt(ms + NORM_EPS) * g_ref[...]).astype(bf16)
    proj = jnp.dot(hn, w_ref[...], preferred_element_type=f32)
    qkw = qw + kw
    qk = proj[:, :qkw]
    qk_swapped = proj[:, qkw:2 * qkw]
    msq = jnp.dot((qk * qk).astype(bf16), bd_ref[...], preferred_element_type=f32) * (1.0 / HEAD_DIM)
    qkr = lax.rsqrt(msq + NORM_EPS) * (qk * c_ref[...] + qk_swapped * s_ref[...])
    q_ref[...] = qkr[:, :qw].astype(bf16)

    lane = lax.broadcasted_iota(jnp.int32, (tr, LANE), 1)
    lo = lane < HEAD_DIM

    def ext(a):
        ar = pltpu.roll(a, HEAD_DIM, axis=1)
        zero = jnp.zeros_like(a)
        return jnp.concatenate([jnp.where(lo, a, zero), jnp.where(lo, zero, ar),
                                jnp.where(lo, ar, zero), jnp.where(lo, zero, a)], axis=1)

    k_ref[...] = ext(qkr[:, qw:qkw]).astype(bf16)
    v_ref[...] = ext(proj[:, 2 * qkw:2 * qkw + kw]).astype(bf16)
    z = proj[:, 2 * qkw + kw:]
    pos = r * tr + lax.broadcasted_iota(jnp.int32, (tr, 1), 0)
    z_ref[...] = jnp.where(pos >= PADF, z, 0.0).astype(bf16)


def _inproj(h2d, g, w_ext, bd, ctab, stab, *, nb, lp, qw, kw, zw):
    d = h2d.shape[1]
    tr = _pick(lp, (384, 128))
    nr = lp // tr
    t = nb * lp
    wcols = w_ext.shape[1]
    qkw = qw + kw
    row = lambda b, r: (b * nr + r, 0)
    const = lambda b, r: (0, 0)
    return pl.pallas_call(
        functools.partial(_inproj_kernel, qw=qw, kw=kw, tr=tr),
        grid=(nb, nr),
        in_specs=[pl.BlockSpec((tr, d), row),
                  pl.BlockSpec((1, d), const),
                  pl.BlockSpec((d, wcols), const),
                  pl.BlockSpec((qkw, qkw), const),
                  pl.BlockSpec((tr, qkw), lambda b, r: (r, 0)),
                  pl.BlockSpec((tr, qkw), lambda b, r: (r, 0))],
        out_specs=[pl.BlockSpec((tr, qw), row),
                   pl.BlockSpec((tr, 4 * kw), row),
                   pl.BlockSpec((tr, 4 * kw), row),
                   pl.BlockSpec((tr, zw), row)],
        out_shape=[jax.ShapeDtypeStruct((t, qw), bf16),
                   jax.ShapeDtypeStruct((t, 4 * kw), bf16),
                   jax.ShapeDtypeStruct((t, 4 * kw), bf16),
                   jax.ShapeDtypeStruct((t, zw), bf16)],
        compiler_params=_cparams(("parallel", "parallel")),
        name="inproj",
    )(h2d, g, w_ext, bd, ctab, stab)


SM_ROWS = 16


def _attn_kernel(q_ref, k_ref, v_ref, g_ref, o_ref, s_a, s_b, p_a, p_b, rl_ref, *, tq):
    lane = lax.broadcasted_iota(jnp.int32, (tq, LANE), 1)
    lo = lane < HEAD_DIM
    real = lax.broadcasted_iota(jnp.int32, (SM_ROWS, LANE), 1) >= PADF
    nt = (((1,), (1,)), ((), ()))
    heads = [(pr, hh) for pr in range(q_ref.shape[1] // LANE) for hh in range(2)]
    s_bufs, p_bufs = (s_a, s_b), (p_a, p_b)

    def scores(n):
        pr, hh = heads[n]
        s_bufs[n % 2][...] = lax.dot_general(q_ref[:, pr * LANE:(pr + 1) * LANE],
                                             k_ref[:, hh * LANE:(hh + 1) * LANE], nt,
                                             preferred_element_type=f32)

    def softmax_rows(s_ref, p_ref, r):
        rows = slice(r * SM_ROWS, (r + 1) * SM_ROWS)
        s0 = jnp.where(real, s_ref[rows, :LANE], -jnp.inf)
        s1 = s_ref[rows, LANE:]
        m = jnp.maximum(jnp.max(s0, axis=-1, keepdims=True), jnp.max(s1, axis=-1, keepdims=True))
        p0 = jnp.exp2(s0 - m)
        p1 = jnp.exp2(s1 - m)
        l = jnp.sum(p0, axis=-1, keepdims=True) + jnp.sum(p1, axis=-1, keepdims=True)
        p_ref[rows, :LANE] = p0.astype(bf16)
        p_ref[rows, LANE:] = p1.astype(bf16)
        rl_ref[rows, :] = jnp.broadcast_to(1.0 / l, (SM_ROWS, LANE))

    scores(0)
    for n, (pr, hh) in enumerate(heads):
        cols = slice(pr * LANE, (pr + 1) * LANE)
        if n + 1 < len(heads):
            scores(n + 1)
        for r in range(tq // SM_ROWS):
            softmax_rows(s_bufs[n % 2], p_bufs[n % 2], r)
        oh = jnp.dot(p_bufs[n % 2][...], v_ref[:, hh * LANE:(hh + 1) * LANE],
                     preferred_element_type=f32) * rl_ref[...]
        if hh == 0:
            o = oh
            continue
        o = o + oh
        o2 = o * o
        s_all = jnp.sum(o2, axis=-1, keepdims=True)
        s_lo = jnp.sum(jnp.where(lo, o2, 0.0), axis=-1, keepdims=True)
        ms = jnp.where(lo, s_lo, s_all - s_lo) * (1.0 / HEAD_DIM)
        o_ref[:, cols] = (o * lax.rsqrt(ms + NORM_EPS) * g_ref[:, cols]).astype(bf16)


def _attention(q, kx, vx, g_out, *, nb, lp, n_kv):
    t, qw = q.shape
    gw = qw // n_kv
    assert gw % LANE == 0
    tq = _pick(lp, (384, 128))
    nq = lp // tq
    return pl.pallas_call(
        functools.partial(_attn_kernel, tq=tq),
        grid=(nb, n_kv, nq),
        in_specs=[pl.BlockSpec((tq, gw), lambda b, h, i: (b * nq + i, h)),
                  pl.BlockSpec((lp, 2 * LANE), lambda b, h, i: (b, h)),
                  pl.BlockSpec((lp, 2 * LANE), lambda b, h, i: (b, h)),
                  pl.BlockSpec((1, gw), lambda b, h, i: (0, h))],
        out_specs=pl.BlockSpec((tq, gw), lambda b, h, i: (b * nq + i, h)),
        out_shape=jax.ShapeDtypeStruct((t, qw), bf16),
        scratch_shapes=[pltpu.VMEM((tq, lp), f32), pltpu.VMEM((tq, lp), f32),
                        pltpu.VMEM((tq, lp), bf16), pltpu.VMEM((tq, lp), bf16),
                        pltpu.VMEM((tq, LANE), f32)],
        compiler_params=_cparams(("parallel", "parallel", "parallel")),
        name="attention",
    )(q, kx, vx, g_out)


def _hyfilter_kernel(bands_ref, w1_ref, b1_ref, sf_ref, w2_ref, b2_ref, w3_ref, dec_ref, o_ref,
                     *, tr, seq_l):
    r = pl.program_id(0)
    hp = lax.Precision.HIGHEST
    t = (r * tr + lax.broadcasted_iota(jnp.int32, (tr, 1), 0)).astype(f32)
    tn = t / (seq_l - 1)
    ang = ((2.0 * math.pi / seq_l) * t) * bands_ref[...]
    lane = lax.broadcasted_iota(jnp.int32, (tr, LANE), 1)
    feat = jnp.where(lane < HY_BANDS, jnp.cos(ang),
                     jnp.where(lane < 2 * HY_BANDS, jnp.sin(ang),
                               jnp.where(lane == 2 * HY_BANDS, tn, 0.0)))
    sf = sf_ref[...]
    h = jnp.sin(sf[0:1] * (jnp.dot(feat, w1_ref[...], precision=hp, preferred_element_type=f32) + b1_ref[...]))
    h = jnp.sin(sf[1:2] * (jnp.dot(h, w2_ref[...], precision=hp, preferred_element_type=f32) + b2_ref[...]))
    h = jnp.dot(h, w3_ref[...], precision=hp, preferred_element_type=f32)
    o_ref[...] = h * jnp.exp(-tn * dec_ref[...])


def _hyfilter(bands, w1p, b1, sf, w2, b2, w3, dec, *, lp, seq_l):
    tr = _pick(lp, (384, 128))
    fh = w2.shape[0]
    wout = w3.shape[1]
    const = lambda r: (0, 0)
    return pl.pallas_call(
        functools.partial(_hyfilter_kernel, tr=tr, seq_l=seq_l),
        grid=(lp // tr,),
        in_specs=[pl.BlockSpec((1, LANE), const), pl.BlockSpec((LANE, fh), const),
                  pl.BlockSpec((1, fh), const), pl.BlockSpec((2, fh), const),
                  pl.BlockSpec((fh, fh), const), pl.BlockSpec((1, fh), const),
                  pl.BlockSpec((fh, wout), const), pl.BlockSpec((1, wout), const)],
        out_specs=pl.BlockSpec((tr, wout), lambda r: (r, 0)),
        out_shape=jax.ShapeDtypeStruct((lp, wout), f32),
        compiler_params=_cparams(("parallel",)),
        name="hyena_filter",
    )(bands, w1p, b1, sf, w2, b2, w3, dec)


HALO = 16


def _hypre_kernel(z_ref, zp_ref, zn_ref, cw_ref, cb_ref, u_ref, x1_ref, *, tr, nr, hw):
    r = pl.program_id(1)
    z = z_ref[...].astype(f32)
    prev_row = jnp.where(r > 0, zp_ref[HALO - 1:HALO, :].astype(f32), 0.0)
    next_row = jnp.where(r < nr - 1, zn_ref[0:1, :].astype(f32), 0.0)
    ridx = lax.broadcasted_iota(jnp.int32, (tr, 1), 0)
    z_prev = jnp.where(ridx == 0, prev_row, pltpu.roll(z, 1, axis=0))
    z_next = jnp.where(ridx == tr - 1, next_row, pltpu.roll(z, tr - 1, axis=0))
    cw = cw_ref[...]
    zc = z_prev * cw[0:1] + z * cw[1:2] + z_next * cw[2:3] + cb_ref[...]
    x0 = zc[:, :hw]
    x1 = zc[:, hw:2 * hw]
    vv = zc[:, 2 * hw:]
    pos = r * tr + ridx
    u_ref[...] = jnp.where(pos >= PADF, vv * x0, 0.0)
    x1_ref[...] = x1


def _hypre(z, cw, cb, *, nb, lp, hw):
    t, zw = z.shape
    tr = _pick(lp, (384, 128))
    nr = lp // tr
    per = tr // HALO
    nh = t // HALO
    row = lambda b, r: (b * nr + r, 0)
    const = lambda b, r: (0, 0)
    return pl.pallas_call(
        functools.partial(_hypre_kernel, tr=tr, nr=nr, hw=hw),
        grid=(nb, nr),
        in_specs=[pl.BlockSpec((tr, zw), row),
                  pl.BlockSpec((HALO, zw), lambda b, r: (jnp.maximum((b * nr + r) * per - 1, 0), 0)),
                  pl.BlockSpec((HALO, zw), lambda b, r: (jnp.minimum((b * nr + r + 1) * per, nh - 1), 0)),
                  pl.BlockSpec((3, zw), const), pl.BlockSpec((1, zw), const)],
        out_specs=[pl.BlockSpec((tr, hw), row), pl.BlockSpec((tr, hw), row)],
        out_shape=[jax.ShapeDtypeStruct((t, hw), f32), jax.ShapeDtypeStruct((t, hw), f32)],
        compiler_params=_cparams(("parallel", "parallel")),
        name="hyena_shortconv",
    )(z, z, z, cw, cb)


def _hyconv_kernel(u_ref, h_ref, y_ref, rl_ref, *, nj, nb):
    jj = lax.broadcasted_iota(jnp.int32, (LANE, LANE), 0)
    ii = lax.broadcasted_iota(jnp.int32, (LANE, LANE), 1)
    upper = ii >= jj
    y_ref[...] = jnp.zeros(y_ref.shape, f32)

    def rolled(r):
        row = jnp.broadcast_to(h_ref[0, r:r + 1, :], (LANE, LANE))
        return pltpu.roll(row, 0, axis=1, stride=1, stride_axis=0)

    rl_ref[...] = rolled(0)
    for d in range(-(nj - 1), nj):
        hi = rolled(d + nj)
        tile = jnp.where(upper, hi, rl_ref[...]).astype(bf16)
        rl_ref[...] = hi
        j0, j1 = max(0, -d), min(nj, nj - d)
        src = u_ref[0, j0 * nb:j1 * nb, :].astype(bf16)
        dst = slice((j0 + d) * nb, (j1 + d) * nb)
        y_ref[0, dst, :] += jnp.dot(src, tile, preferred_element_type=f32)


def _hyconv(ut, hc, *, nj, nb):
    c, rows, _ = ut.shape
    return pl.pallas_call(
        functools.partial(_hyconv_kernel, nj=nj, nb=nb),
        grid=(c,),
        in_specs=[pl.BlockSpec((1, rows, LANE), lambda i: (i, 0, 0)),
                  pl.BlockSpec((1, 2 * nj, LANE), lambda i: (i, 0, 0))],
        out_specs=pl.BlockSpec((1, rows, LANE), lambda i: (i, 0, 0)),
        out_shape=jax.ShapeDtypeStruct((c, rows, LANE), f32),
        scratch_shapes=[pltpu.VMEM((LANE, LANE), f32)],
        compiler_params=_cparams(("parallel",)),
        name="hyena_longconv",
    )(ut, hc)


def _outproj_kernel(h_ref, oa_ref, y_ref, u_ref, x1_ref, ds_ref, gh_ref, bd_ref, w_ref, o_ref):
    ohy = (y_ref[...] + u_ref[...] * ds_ref[...]) * x1_ref[...]
    ms = jnp.dot((ohy * ohy).astype(bf16), bd_ref[...], preferred_element_type=f32) * (1.0 / HEAD_DIM)
    ohn = (ohy * lax.rsqrt(ms + NORM_EPS) * gh_ref[...]).astype(bf16)
    mixed = jnp.concatenate([oa_ref[...], ohn], axis=1)
    o_ref[...] = h_ref[...] + jnp.dot(mixed, w_ref[...], preferred_element_type=f32)


def _outproj(h2d, oa, y, u, x1, dskip, gh, bd, w_out):
    t, d = h2d.shape
    aw = oa.shape[1]
    hw = y.shape[1]
    tr = _pick(t, (384, 128))
    row = lambda r: (r, 0)
    const = lambda r: (0, 0)
    return pl.pallas_call(
        _outproj_kernel,
        grid=(t // tr,),
        in_specs=[pl.BlockSpec((tr, d), row), pl.BlockSpec((tr, aw), row),
                  pl.BlockSpec((tr, hw), row), pl.BlockSpec((tr, hw), row), pl.BlockSpec((tr, hw), row),
                  pl.BlockSpec((1, hw), const), pl.BlockSpec((1, hw), const),
                  pl.BlockSpec((hw, hw), const), pl.BlockSpec((d, d), const)],
        out_specs=pl.BlockSpec((tr, d), row),
        out_shape=jax.ShapeDtypeStruct((t, d), f32),
        compiler_params=_cparams(("parallel",)),
        name="outproj",
    )(h2d, oa, y, u, x1, dskip, gh, bd, w_out)


NTOP = PEER_TOPK + 1
TOPPAD = 24


def _gelu_x2(x):
    return x * (1.0 + lax.erf(x * (1.0 / math.sqrt(2.0))))


SUBLANES = 8


def _sort_network(n):
    def merge(lo, hi, r):
        step = r * 2
        if step < hi - lo:
            yield from merge(lo, hi, step)
            yield from merge(lo + r, hi, step)
            for i in range(lo + r, hi - r, step):
                yield (i, i + r)
        else:
            yield (lo, lo + r)

    def sort(lo, hi):
        if hi - lo >= 1:
            mid = lo + (hi - lo) // 2
            yield from sort(lo, mid)
            yield from sort(mid + 1, hi)
            yield from merge(lo, hi, 1)

    return list(sort(0, n - 1))


def _top_values(s, top_ref, hp, cols):
    nt = s.shape[0] // SUBLANES
    v = [s[i * SUBLANES:(i + 1) * SUBLANES, :] for i in range(nt)]
    for i, j in _sort_network(nt):
        v[i], v[j] = jnp.maximum(v[i], v[j]), jnp.minimum(v[i], v[j])
    v.append(jnp.full((SUBLANES, LANE), -jnp.inf, f32))
    for t in range(NTOP):
        m = jnp.max(v[0], axis=0, keepdims=True)
        top_ref[hp, t:t + 1, cols] = m
        pop = v[0] == m
        for k in range(min(nt, NTOP - 1 - t)):
            v[k] = jnp.where(pop, v[k + 1], v[k])


CAND_SUB = 8
CAND_ROWS = TOPPAD + (CAND_SUB - 1) * CAND_SUB + 2 * CAND_SUB


def _peer_kernel(h_ref, g_ref, wq_ref, keys_ref, u_ref, vt_ref, o_ref,
                 hnt_ref, e1_ref, e2_ref, th_ref, top_ref, cand_ref, acc_ref,
                 act_ref, wa_ref, *, tb, eb, nheads, nkeys, ne):
    e = pl.program_id(1)
    neg = -jnp.inf

    @pl.when(e == 0)
    def _route():
        x = h_ref[...]
        ms = jnp.mean(x * x, axis=-1, keepdims=True)
        hn = x * lax.rsqrt(ms + NORM_EPS) * g_ref[...]
        hnt = hn.T.astype(bf16)
        hnt_ref[...] = hnt
        qt = jnp.dot(wq_ref[...], hnt, preferred_element_type=f32)
        dh = keys_ref.shape[2]
        for hp in range(2 * nheads):
            head, half = divmod(hp, 2)
            s = jnp.dot(keys_ref[hp], qt[hp * dh:(hp + 1) * dh, :].astype(bf16),
                        preferred_element_type=f32)
            for c in range(tb // LANE):
                _top_values(s[:, c * LANE:(c + 1) * LANE], top_ref, hp, slice(c * LANE, (c + 1) * LANE))
            top_ref[hp, NTOP:, :] = jnp.full((TOPPAD - NTOP, tb), neg, f32)
            smax = top_ref[hp, 0:1, :]
            ex = jnp.exp(s - smax)
            if half == 0:
                e1_ref[head] = ex
            else:
                e2_ref[head] = ex.astype(bf16)
            top_ref[hp] = jnp.exp(top_ref[hp] - smax)
        for head in range(nheads):
            ta, tb_ = 2 * head, 2 * head + 1
            cand_ref[0:TOPPAD, :] = top_ref[ta, 0:1, :] * top_ref[tb_]
            for k1 in range(1, CAND_SUB):
                r0 = TOPPAD + (k1 - 1) * CAND_SUB
                cand_ref[r0:r0 + CAND_SUB, :] = top_ref[ta, k1:k1 + 1, :] * top_ref[tb_, 0:CAND_SUB, :]
            r0 = TOPPAD + (CAND_SUB - 1) * CAND_SUB
            cand_ref[r0:, :] = top_ref[ta, CAND_SUB:, :] * top_ref[tb_, 0:1, :]
            tops = []
            for _ in range(NTOP):
                w = cand_ref[...]
                m = jnp.max(w, axis=0, keepdims=True)
                tops.append(m)
                cand_ref[...] = jnp.where(w == m, -1.0, w)
            zsum = tops[0]
            for k in range(1, PEER_TOPK):
                zsum = zsum + jnp.maximum(tops[k], 0.0)
            inv = 0.5 / zsum
            e1_ref[head] = e1_ref[head] * inv
            mid = 0.5 * (jnp.maximum(tops[PEER_TOPK - 1], 0.0) + jnp.maximum(tops[PEER_TOPK], 0.0))
            th_ref[head:head + 1, :] = mid * inv
        acc_ref[...] = jnp.zeros(acc_ref.shape, f32)

    ipb = eb // nkeys
    i0 = pl.multiple_of(e * ipb, ipb)
    nchunk = tb // LANE
    npiece = eb // MXU_K
    rows_per_piece = MXU_K // nkeys

    def routing_weights(ii, c):
        cols = slice(c * LANE, (c + 1) * LANE)
        w = None
        for head in range(nheads):
            e1row = e1_ref[head, pl.ds(i0, ipb), cols][ii:ii + 1, :].astype(bf16)
            p = e1row * e2_ref[head, :, cols]
            kept = jnp.where(p >= th_ref[head:head + 1, cols].astype(bf16), p, jnp.zeros_like(p))
            w = kept if w is None else w + kept
        return w

    act_ref[...] = jnp.dot(u_ref[...], hnt_ref[...], preferred_element_type=f32)
    for k in range(npiece):
        ks = slice(k * MXU_K, (k + 1) * MXU_K)
        for ii in range(k * rows_per_piece, (k + 1) * rows_per_piece):
            rows = slice(ii * nkeys, (ii + 1) * nkeys)
            for c in range(nchunk):
                cols = slice(c * LANE, (c + 1) * LANE)
                w = routing_weights(ii, c)
                wa_ref[rows, cols] = w * _gelu_x2(act_ref[rows, cols]).astype(bf16)
        acc_ref[...] += jnp.dot(vt_ref[:, ks], wa_ref[ks, :], preferred_element_type=f32)

    @pl.when(e == ne - 1)
    def _finish():
        o_ref[...] = h_ref[...] + acc_ref[...].T


def _peer(h2d, g, wq_t, keys, u_tab, vt_tab, *, nheads):
    t, d = h2d.shape
    nexp = u_tab.shape[0]
    nkeys = keys.shape[1]
    tb = _pick(t, (512, 384, 128))
    eb = 16 * nkeys
    qrows = wq_t.shape[0]
    ne = nexp // eb
    tok = lambda i, e: (i, 0)
    const2 = lambda i, e: (0, 0)
    return pl.pallas_call(
        functools.partial(_peer_kernel, tb=tb, eb=eb, nheads=nheads, nkeys=nkeys, ne=ne),
        grid=(t // tb, ne),
        in_specs=[pl.BlockSpec((tb, d), tok),
                  pl.BlockSpec((1, d), const2),
                  pl.BlockSpec((qrows, d), const2),
                  pl.BlockSpec(keys.shape, lambda i, e: (0, 0, 0)),
                  pl.BlockSpec((eb, d), lambda i, e: (e, 0)),
                  pl.BlockSpec((d, eb), lambda i, e: (0, e))],
        out_specs=pl.BlockSpec((tb, d), tok),
        out_shape=jax.ShapeDtypeStruct((t, d), f32),
        scratch_shapes=[pltpu.VMEM((d, tb), bf16),
                        pltpu.VMEM((nheads, nkeys, tb), f32),
                        pltpu.VMEM((nheads, nkeys, tb), bf16),
                        pltpu.VMEM((nheads, tb), f32),
                        pltpu.VMEM((2 * nheads, TOPPAD, tb), f32),
                        pltpu.VMEM((CAND_ROWS, tb), f32),
                        pltpu.VMEM((d, tb), f32),
                        pltpu.VMEM((eb, tb), f32),
                        pltpu.VMEM((eb, tb), bf16)],
        compiler_params=_cparams(("parallel", "arbitrary")),
        name="peer",
    )(h2d, g, wq_t, keys, u_tab, vt_tab)


def _final_kernel(h_ref, g_ref, o_ref):
    x = h_ref[0]
    ms = jnp.mean(x * x, axis=-1, keepdims=True)
    o_ref[0] = x * lax.rsqrt(ms + NORM_EPS) * g_ref[...]


def _final(h3d, g, *, b0, nbatch, s):
    _, lp, d = h3d.shape
    tr = LANE
    return pl.pallas_call(
        _final_kernel,
        grid=(nbatch, s // tr),
        in_specs=[pl.BlockSpec((1, tr, d), lambda b, r: (b0 + b, r + 1, 0)),
                  pl.BlockSpec((1, d), lambda b, r: (0, 0))],
        out_specs=pl.BlockSpec((1, tr, d), lambda b, r: (b, r, 0)),
        out_shape=jax.ShapeDtypeStruct((nbatch, s, d), f32),
        compiler_params=_cparams(("parallel", "parallel")),
        name="final_norm",
    )(h3d, g)


def _block_diag(n):
    a = jnp.arange(n) // HEAD_DIM
    return (a[:, None] == a[None, :]).astype(bf16)


def _rope_tables(s, lp, gq, gk, n_heads, n_kv):
    rows = s // GRID_W
    row = jnp.repeat(jnp.arange(rows, dtype=f32), GRID_W)
    col = jnp.tile(jnp.arange(GRID_W, dtype=f32), rows)
    per_axis = HEAD_DIM // 4
    inv = ROPE_THETA ** (-jnp.arange(per_axis, dtype=f32) / per_axis)
    ang = jnp.concatenate([row[:, None] * inv, col[:, None] * inv], axis=-1)
    ang = jnp.concatenate([jnp.zeros((lp - s, HEAD_DIM // 2), f32), ang], axis=0)
    cos, sin = jnp.cos(ang), jnp.sin(ang)
    cfull = jnp.repeat(cos, 2, axis=1)
    sfull = jnp.stack([-sin, sin], axis=-1).reshape(lp, HEAD_DIM)
    swap = jnp.arange(HEAD_DIM) ^ 1
    scale = HEAD_DIM ** -0.5 * math.log2(math.e)
    cq, sq = cfull * gq * scale, sfull * gq[swap] * scale
    ck, sk = cfull * gk, sfull * gk[swap]
    ctab = jnp.concatenate([jnp.tile(cq, (1, n_heads)), jnp.tile(ck, (1, n_kv))], axis=1)
    stab = jnp.concatenate([jnp.tile(sq, (1, n_heads)), jnp.tile(sk, (1, n_kv))], axis=1)
    return ctab, stab


def kernel(x_prompt, x_sample, meta_tokens, norm1_g, w_in, q_norm_g, k_norm_g, hy_conv_w, hy_conv_b, hy_ffn_w1, hy_ffn_b1, hy_sin_freq, hy_ffn_w2, hy_ffn_b2, hy_ffn_w3, hy_decay, hy_dskip, attn_out_g, hy_out_g, w_out, norm2_g, peer_wq, peer_keys, peer_u, peer_v, final_g):
    depth, d, _ = w_in.shape
    b_p, s, _ = x_prompt.shape
    b_s = x_sample.shape[0]
    assert x_sample.shape[1] == s and s % LANE == 0 and s % GRID_W == 0
    nb = b_p + b_s
    seq_l = s + N_META
    lp = s + LANE
    nj = lp // LANE
    t = nb * lp
    n_heads = attn_out_g.shape[1]
    aw = n_heads * HEAD_DIM
    hw = d - aw
    kw = (w_in.shape[2] - aw - 3 * hw) // 2
    n_kv = kw // HEAD_DIM
    assert kw == LANE, "k/v lane-pair layout assumes two kv heads"
    peer_heads = peer_keys.shape[2]
    nkeys = peer_keys.shape[3]
    assert nkeys == LANE

    x = jnp.concatenate([x_prompt, x_sample], axis=0)
    front = jnp.concatenate([jnp.zeros((PADF, d), f32), meta_tokens.astype(f32)], axis=0)
    h = jnp.concatenate([jnp.broadcast_to(front[None], (nb, LANE, d)), x], axis=1).reshape(t, d)

    bd_qk = _block_diag(aw + kw)
    bd_hy = _block_diag(hw)
    swap_q = jnp.arange(aw) ^ 1
    swap_k = jnp.arange(kw) ^ 1
    bands = jnp.linspace(1e-4, HY_BANDS - 1, HY_BANDS, dtype=f32)
    bands_p = jnp.zeros((1, LANE), f32).at[0, :HY_BANDS].set(bands).at[0, HY_BANDS:2 * HY_BANDS].set(bands)

    for l in range(depth):
        w = w_in[l]
        wq, wk = w[:, :aw], w[:, aw:aw + kw]
        wv, wz = w[:, aw + kw:aw + 2 * kw], w[:, aw + 2 * kw:]
        w_ext = jnp.concatenate([wq, wk, wq[:, swap_q], wk[:, swap_k], wv, wz], axis=1).astype(bf16)
        ctab, stab = _rope_tables(s, lp, q_norm_g[l], k_norm_g[l], n_heads, n_kv)
        q, kx, vx, z = _inproj(h, norm1_g[l][None, :], w_ext, bd_qk, ctab, stab,
                               nb=nb, lp=lp, qw=aw, kw=kw, zw=3 * hw)
        o_att = _attention(q, kx, vx, attn_out_g[l].reshape(1, aw), nb=nb, lp=lp, n_kv=n_kv)

        w1 = hy_ffn_w1[l]
        fh = w1.shape[1]
        w1p = jnp.zeros((LANE, fh), f32).at[:2 * HY_BANDS].set(w1[1:]).at[2 * HY_BANDS].set(w1[0])
        filt = _hyfilter(bands_p, w1p, hy_ffn_b1[l][None], hy_sin_freq[l], hy_ffn_w2[l], hy_ffn_b2[l][None],
                         hy_ffn_w3[l], hy_decay[l].reshape(1, 2 * hw), lp=lp, seq_l=seq_l)
        hf, hb = filt[:seq_l, :hw], filt[:seq_l, hw:]
        lags = jnp.concatenate([jnp.zeros((lp - seq_l + 1, hw), f32), hb[:0:-1], hf,
                                jnp.zeros((lp - seq_l, hw), f32)], axis=0)
        hc = lags.T.reshape(hw, 2 * nj, LANE)

        u, x1 = _hypre(z, hy_conv_w[l], hy_conv_b[l][None], nb=nb, lp=lp, hw=hw)
        ut = u.reshape(nb, nj, LANE, hw).transpose(3, 1, 0, 2).reshape(hw, nj * nb, LANE)
        yt = _hyconv(ut, hc, nj=nj, nb=nb)
        y = yt.reshape(hw, nj, nb, LANE).transpose(2, 1, 3, 0).reshape(t, hw)

        h = _outproj(h, o_att, y, u, x1, hy_dskip[l][None], hy_out_g[l].reshape(1, hw), bd_hy,
                     w_out[l].astype(bf16))

        dh = peer_keys.shape[4]
        keys = peer_keys[l].transpose(1, 0, 2, 3).reshape(2 * peer_heads, nkeys, dh).astype(bf16)
        h = _peer(h, norm2_g[l][None], peer_wq[l].T.astype(bf16), keys,
                  peer_u[l].astype(bf16), peer_v[l].T.astype(bf16), nheads=peer_heads)

    h3 = h.reshape(nb, lp, d)
    y_prompt = _final(h3, final_g[None], b0=0, nbatch=b_p, s=s)
    y_sample = _final(h3, final_g[None], b0=b_p, nbatch=b_s, s=s)
    return (y_prompt, y_sample)
```

```python
import functools
import math

import jax
import jax.numpy as jnp
from jax import lax
from jax.experimental import pallas as pl
from jax.experimental.pallas import tpu as pltpu

N_META = 16
GRID_W = 64
HEAD_DIM = 64
ROPE_THETA = 10000.0
HY_BANDS = 16
PEER_TOPK = 16
NORM_EPS = 1e-6
LANE = 128
MXU_K = 256
PADF = LANE - N_META
VMEM_LIMIT = 56 * 1024 * 1024

f32 = jnp.float32
bf16 = jnp.bfloat16


def _cparams(sem):
    return pltpu.CompilerParams(dimension_semantics=sem, vmem_limit_bytes=VMEM_LIMIT)


def _pick(n, cands):
    for c in cands:
        if n % c == 0:
            return c
    raise ValueError(f"no block size for {n}")


def _inproj_kernel(x_ref, g_ref, w_ref, bd_ref, c_ref, s_ref,
                   q_ref, k_ref, v_ref, z_ref, *, qw, kw, tr):
    r = pl.program_id(1)
    x = x_ref[...]
    ms = jnp.mean(x * x, axis=-1, keepdims=True)
    hn = (x * lax.rsqrt(ms + NORM_EPS) * g_ref[...]).astype(bf16)
    proj = jnp.dot(hn, w_ref[...], preferred_element_type=f32)
    qkw = qw + kw
    qk = proj[:, :qkw]
    qk_swapped = proj[:, qkw:2 * qkw]
    msq = jnp.dot((qk * qk).astype(bf16), bd_ref[...], preferred_element_type=f32) * (1.0 / HEAD_DIM)
    qkr = lax.rsqrt(msq + NORM_EPS) * (qk * c_ref[...] + qk_swapped * s_ref[...])
    q_ref[...] = qkr[:, :qw].astype(bf16)

    lane = lax.broadcasted_iota(jnp.int32, (tr, LANE), 1)
    lo = lane < HEAD_DIM

    def ext(a):
        ar = pltpu.roll(a, HEAD_DIM, axis=1)
        zero = jnp.zeros_like(a)
        return jnp.concatenate([jnp.where(lo, a, zero), jnp.where(lo, zero, ar),
                                jnp.where(lo, ar, zero), jnp.where(lo, zero, a)], axis=1)

    k_ref[...] = ext(qkr[:, qw:qkw]).astype(bf16)
    v_ref[...] = ext(proj[:, 2 * qkw:2 * qkw + kw]).astype(bf16)
    z = proj[:, 2 * qkw + kw:]
    pos = r * tr + lax.broadcasted_iota(jnp.int32, (tr, 1), 0)
    z_ref[...] = jnp.where(pos >= PADF, z, 0.0).astype(bf16)


def _inproj(h2d, g, w_ext, bd, ctab, stab, *, nb, lp, qw, kw, zw):
    d = h2d.shape[1]
    tr = _pick(lp, (384, 128))
    nr = lp // tr
    t = nb * lp
    wcols = w_ext.shape[1]
    qkw = qw + kw
    row = lambda b, r: (b * nr + r, 0)
    const = lambda b, r: (0, 0)
    return pl.pallas_call(
        functools.partial(_inproj_kernel, qw=qw, kw=kw, tr=tr),
        grid=(nb, nr),
        in_specs=[pl.BlockSpec((tr, d), row),
                  pl.BlockSpec((1, d), const),
                  pl.BlockSpec((d, wcols), const),
                  pl.BlockSpec((qkw, qkw), const),
                  pl.BlockSpec((tr, qkw), lambda b, r: (r, 0)),
                  pl.BlockSpec((tr, qkw), lambda b, r: (r, 0))],
        out_specs=[pl.BlockSpec((tr, qw), row),
                   pl.BlockSpec((tr, 4 * kw), row),
                   pl.BlockSpec((tr, 4 * kw), row),
                   pl.BlockSpec((tr, zw), row)],
        out_shape=[jax.ShapeDtypeStruct((t, qw), bf16),
                   jax.ShapeDtypeStruct((t, 4 * kw), bf16),
                   jax.ShapeDtypeStruct((t, 4 * kw), bf16),
                   jax.ShapeDtypeStruct((t, zw), bf16)],
        compiler_params=_cparams(("parallel", "parallel")),
        name="inproj",
    )(h2d, g, w_ext, bd, ctab, stab)


SM_ROWS = 16


def _attn_kernel(q_ref, k_ref, v_ref, g_ref, o_ref, s_a, s_b, p_a, p_b, rl_ref, *, tq):
    lane = lax.broadcasted_iota(jnp.int32, (tq, LANE), 1)
    lo = lane < HEAD_DIM
    real = lax.broadcasted_iota(jnp.int32, (SM_ROWS, LANE), 1) >= PADF
    nt = (((1,), (1,)), ((), ()))
    heads = [(pr, hh) for pr in range(q_ref.shape[1] // LANE) for hh in range(2)]
    s_bufs, p_bufs = (s_a, s_b), (p_a, p_b)

    def scores(n):
        pr, hh = heads[n]
        s_bufs[n % 2][...] = lax.dot_general(q_ref[:, pr * LANE:(pr + 1) * LANE],
                                             k_ref[:, hh * LANE:(hh + 1) * LANE], nt,
                                             preferred_element_type=f32)

    def softmax_rows(s_ref, p_ref, r):
        rows = slice(r * SM_ROWS, (r + 1) * SM_ROWS)
        s0 = jnp.where(real, s_ref[rows, :LANE], -jnp.inf)
        s1 = s_ref[rows, LANE:]
        m = jnp.maximum(jnp.max(s0, axis=-1, keepdims=True), jnp.max(s1, axis=-1, keepdims=True))
        p0 = jnp.exp2(s0 - m)
        p1 = jnp.exp2(s1 - m)
        l = jnp.sum(p0, axis=-1, keepdims=True) + jnp.sum(p1, axis=-1, keepdims=True)
        p_ref[rows, :LANE] = p0.astype(bf16)
        p_ref[rows, LANE:] = p1.astype(bf16)
        rl_ref[rows, :] = jnp.broadcast_to(1.0 / l, (SM_ROWS, LANE))

    scores(0)
    for n, (pr, hh) in enumerate(heads):
        cols = slice(pr * LANE, (pr + 1) * LANE)
        if n + 1 < len(heads):
            scores(n + 1)
        for r in range(tq // SM_ROWS):
            softmax_rows(s_bufs[n % 2], p_bufs[n % 2], r)
        oh = jnp.dot(p_bufs[n % 2][...], v_ref[:, hh * LANE:(hh + 1) * LANE],
                     preferred_element_type=f32) * rl_ref[...]
        if hh == 0:
            o = oh
            continue
        o = o + oh
        o2 = o * o
        s_all = jnp.sum(o2, axis=-1, keepdims=True)
        s_lo = jnp.sum(jnp.where(lo, o2, 0.0), axis=-1, keepdims=True)
        ms = jnp.where(lo, s_lo, s_all - s_lo) * (1.0 / HEAD_DIM)
        o_ref[:, cols] = (o * lax.rsqrt(ms + NORM_EPS) * g_ref[:, cols]).astype(bf16)


def _attention(q, kx, vx, g_out, *, nb, lp, n_kv):
    t, qw = q.shape
    gw = qw // n_kv
    assert gw % LANE == 0
    tq = _pick(lp, (384, 128))
    nq = lp // tq
    return pl.pallas_call(
        functools.partial(_attn_kernel, tq=tq),
        grid=(nb, n_kv, nq),
        in_specs=[pl.BlockSpec((tq, gw), lambda b, h, i: (b * nq + i, h)),
                  pl.BlockSpec((lp, 2 * LANE), lambda b, h, i: (b, h)),
                  pl.BlockSpec((lp, 2 * LANE), lambda b, h, i: (b, h)),
                  pl.BlockSpec((1, gw), lambda b, h, i: (0, h))],
        out_specs=pl.BlockSpec((tq, gw), lambda b, h, i: (b * nq + i, h)),
        out_shape=jax.ShapeDtypeStruct((t, qw), bf16),
        scratch_shapes=[pltpu.VMEM((tq, lp), f32), pltpu.VMEM((tq, lp), f32),
                        pltpu.VMEM((tq, lp), bf16), pltpu.VMEM((tq, lp), bf16),
                        pltpu.VMEM((tq, LANE), f32)],
        compiler_params=_cparams(("parallel", "parallel", "parallel")),
        name="attention",
    )(q, kx, vx, g_out)


def _hyfilter_kernel(bands_ref, w1_ref, b1_ref, sf_ref, w2_ref, b2_ref, w3_ref, dec_ref, o_ref,
                     *, tr, seq_l):
    r = pl.program_id(0)
    hp = lax.Precision.HIGHEST
    t = (r * tr + lax.broadcasted_iota(jnp.int32, (tr, 1), 0)).astype(f32)
    tn = t / (seq_l - 1)
    ang = ((2.0 * math.pi / seq_l) * t) * bands_ref[...]
    lane = lax.broadcasted_iota(jnp.int32, (tr, LANE), 1)
    feat = jnp.where(lane < HY_BANDS, jnp.cos(ang),
                     jnp.where(lane < 2 * HY_BANDS, jnp.sin(ang),
                               jnp.where(lane == 2 * HY_BANDS, tn, 0.0)))
    sf = sf_ref[...]
    h = jnp.sin(sf[0:1] * (jnp.dot(feat, w1_ref[...], precision=hp, preferred_element_type=f32) + b1_ref[...]))
    h = jnp.sin(sf[1:2] * (jnp.dot(h, w2_ref[...], precision=hp, preferred_element_type=f32) + b2_ref[...]))
    h = jnp.dot(h, w3_ref[...], precision=hp, preferred_element_type=f32)
    o_ref[...] = h * jnp.exp(-tn * dec_ref[...])


def _hyfilter(bands, w1p, b1, sf, w2, b2, w3, dec, *, lp, seq_l):
    tr = _pick(lp, (384, 128))
    fh = w2.shape[0]
    wout = w3.shape[1]
    const = lambda r: (0, 0)
    return pl.pallas_call(
        functools.partial(_hyfilter_kernel, tr=tr, seq_l=seq_l),
        grid=(lp // tr,),
        in_specs=[pl.BlockSpec((1, LANE), const), pl.BlockSpec((LANE, fh), const),
                  pl.BlockSpec((1, fh), const), pl.BlockSpec((2, fh), const),
                  pl.BlockSpec((fh, fh), const), pl.BlockSpec((1, fh), const),
                  pl.BlockSpec((fh, wout), const), pl.BlockSpec((1, wout), const)],
        out_specs=pl.BlockSpec((tr, wout), lambda r: (r, 0)),
        out_shape=jax.ShapeDtypeStruct((lp, wout), f32),
        compiler_params=_cparams(("parallel",)),
        name="hyena_filter",
    )(bands, w1p, b1, sf, w2, b2, w3, dec)


HALO = 16


def _hypre_kernel(z_ref, zp_ref, zn_ref, cw_ref, cb_ref, u_ref, x1_ref, *, tr, nr, hw):
    r = pl.program_id(1)
    z = z_ref[...].astype(f32)
    prev_row = jnp.where(r > 0, zp_ref[HALO - 1:HALO, :].astype(f32), 0.0)
    next_row = jnp.where(r < nr - 1, zn_ref[0:1, :].astype(f32), 0.0)
    ridx = lax.broadcasted_iota(jnp.int32, (tr, 1), 0)
    z_prev = jnp.where(ridx == 0, prev_row, pltpu.roll(z, 1, axis=0))
    z_next = jnp.where(ridx == tr - 1, next_row, pltpu.roll(z, tr - 1, axis=0))
    cw = cw_ref[...]
    zc = z_prev * cw[0:1] + z * cw[1:2] + z_next * cw[2:3] + cb_ref[...]
    x0 = zc[:, :hw]
    x1 = zc[:, hw:2 * hw]
    vv = zc[:, 2 * hw:]
    pos = r * tr + ridx
    u_ref[...] = jnp.where(pos >= PADF, vv * x0, 0.0)
    x1_ref[...] = x1


def _hypre(z, cw, cb, *, nb, lp, hw):
    t, zw = z.shape
    tr = _pick(lp, (384, 128))
    nr = lp // tr
    per = tr // HALO
    nh = t // HALO
    row = lambda b, r: (b * nr + r, 0)
    const = lambda b, r: (0, 0)
    return pl.pallas_call(
        functools.partial(_hypre_kernel, tr=tr, nr=nr, hw=hw),
        grid=(nb, nr),
        in_specs=[pl.BlockSpec((tr, zw), row),
                  pl.BlockSpec((HALO, zw), lambda b, r: (jnp.maximum((b * nr + r) * per - 1, 0), 0)),
                  pl.BlockSpec((HALO, zw), lambda b, r: (jnp.minimum((b * nr + r + 1) * per, nh - 1), 0)),
                  pl.BlockSpec((3, zw), const), pl.BlockSpec((1, zw), const)],
        out_specs=[pl.BlockSpec((tr, hw), row), pl.BlockSpec((tr, hw), row)],
        out_shape=[jax.ShapeDtypeStruct((t, hw), f32), jax.ShapeDtypeStruct((t, hw), f32)],
        compiler_params=_cparams(("parallel", "parallel")),
        name="hyena_shortconv",
    )(z, z, z, cw, cb)


def _hyconv_kernel(u_ref, h_ref, y_ref, rl_ref, *, nj, nb):
    jj = lax.broadcasted_iota(jnp.int32, (LANE, LANE), 0)
    ii = lax.broadcasted_iota(jnp.int32, (LANE, LANE), 1)
    upper = ii >= jj
    y_ref[...] = jnp.zeros(y_ref.shape, f32)

    def rolled(r):
        row = jnp.broadcast_to(h_ref[0, r:r + 1, :], (LANE, LANE))
        return pltpu.roll(row, 0, axis=1, stride=1, stride_axis=0)

    rl_ref[...] = rolled(0)
    for d in range(-(nj - 1), nj):
        hi = rolled(d + nj)
        tile = jnp.where(upper, hi, rl_ref[...]).astype(bf16)
        rl_ref[...] = hi
        j0, j1 = max(0, -d), min(nj, nj - d)
        src = u_ref[0, j0 * nb:j1 * nb, :].astype(bf16)
        dst = slice((j0 + d) * nb, (j1 + d) * nb)
        y_ref[0, dst, :] += jnp.dot(src, tile, preferred_element_type=f32)


def _hyconv(ut, hc, *, nj, nb):
    c, rows, _ = ut.shape
    return pl.pallas_call(
        functools.partial(_hyconv_kernel, nj=nj, nb=nb),
        grid=(c,),
        in_specs=[pl.BlockSpec((1, rows, LANE), lambda i: (i, 0, 0)),
                  pl.BlockSpec((1, 2 * nj, LANE), lambda i: (i, 0, 0))],
        out_specs=pl.BlockSpec((1, rows, LANE), lambda i: (i, 0, 0)),
        out_shape=jax.ShapeDtypeStruct((c, rows, LANE), f32),
        scratch_shapes=[pltpu.VMEM((LANE, LANE), f32)],
        compiler_params=_cparams(("parallel",)),
        name="hyena_longconv",
    )(ut, hc)


def _outproj_kernel(h_ref, oa_ref, y_ref, u_ref, x1_ref, ds_ref, gh_ref, bd_ref, w_ref, o_ref):
    ohy = (y_ref[...] + u_ref[...] * ds_ref[...]) * x1_ref[...]
    ms = jnp.dot((ohy * ohy).astype(bf16), bd_ref[...], preferred_element_type=f32) * (1.0 / HEAD_DIM)
    ohn = (ohy * lax.rsqrt(ms + NORM_EPS) * gh_ref[...]).astype(bf16)
    mixed = jnp.concatenate([oa_ref[...], ohn], axis=1)
    o_ref[...] = h_ref[...] + jnp.dot(mixed, w_ref[...], preferred_element_type=f32)


def _outproj(h2d, oa, y, u, x1, dskip, gh, bd, w_out):
    t, d = h2d.shape
    aw = oa.shape[1]
    hw = y.shape[1]
    tr = _pick(t, (384, 128))
    row = lambda r: (r, 0)
    const = lambda r: (0, 0)
    return pl.pallas_call(
        _outproj_kernel,
        grid=(t // tr,),
        in_specs=[pl.BlockSpec((tr, d), row), pl.BlockSpec((tr, aw), row),
                  pl.BlockSpec((tr, hw), row), pl.BlockSpec((tr, hw), row), pl.BlockSpec((tr, hw), row),
                  pl.BlockSpec((1, hw), const), pl.BlockSpec((1, hw), const),
                  pl.BlockSpec((hw, hw), const), pl.BlockSpec((d, d), const)],
        out_specs=pl.BlockSpec((tr, d), row),
        out_shape=jax.ShapeDtypeStruct((t, d), f32),
        compiler_params=_cparams(("parallel",)),
        name="outproj",
    )(h2d, oa, y, u, x1, dskip, gh, bd, w_out)


NTOP = PEER_TOPK + 1
TOPPAD = 24


def _gelu_x2(x):
    return x * (1.0 + lax.erf(x * (1.0 / math.sqrt(2.0))))


SUBLANES = 8


def _sort_network(n):
    def merge(lo, hi, r):
        step = r * 2
        if step < hi - lo:
            yield from merge(lo, hi, step)
            yield from merge(lo + r, hi, step)
            for i in range(lo + r, hi - r, step):
                yield (i, i + r)
        else:
            yield (lo, lo + r)

    def sort(lo, hi):
        if hi - lo >= 1:
            mid = lo + (hi - lo) // 2
            yield from sort(lo, mid)
            yield from sort(mid + 1, hi)
            yield from merge(lo, hi, 1)

    return list(sort(0, n - 1))


def _top_values(s, top_ref, hp, cols):
    nt = s.shape[0] // SUBLANES
    v = [s[i * SUBLANES:(i + 1) * SUBLANES, :] for i in range(nt)]
    for i, j in _sort_network(nt):
        v[i], v[j] = jnp.maximum(v[i], v[j]), jnp.minimum(v[i], v[j])
    v.append(jnp.full((SUBLANES, LANE), -jnp.inf, f32))
    for t in range(NTOP):
        m = jnp.max(v[0], axis=0, keepdims=True)
        top_ref[hp, t:t + 1, cols] = m
        pop = v[0] == m
        for k in range(min(nt, NTOP - 1 - t)):
            v[k] = jnp.where(pop, v[k + 1], v[k])


CAND_SUB = 8
CAND_ROWS = TOPPAD + (CAND_SUB - 1) * CAND_SUB + 2 * CAND_SUB


def _peer_kernel(h_ref, g_ref, wq_ref, keys_ref, u_ref, vt_ref, o_ref,
                 hnt_ref, e1_ref, e2_ref, th_ref, top_ref, cand_ref, acc_ref,
                 act_ref, wa_ref, *, tb, eb, nheads, nkeys, ne):
    e = pl.program_id(1)
    neg = -jnp.inf

    @pl.when(e == 0)
    def _route():
        x = h_ref[...]
        ms = jnp.mean(x * x, axis=-1, keepdims=True)
        hn = x * lax.rsqrt(ms + NORM_EPS) * g_ref[...]
        hnt = hn.T.astype(bf16)
        hnt_ref[...] = hnt
        qt = jnp.dot(wq_ref[...], hnt, preferred_element_type=f32)
        dh = keys_ref.shape[2]
        for hp in range(2 * nheads):
            head, half = divmod(hp, 2)
            s = jnp.dot(keys_ref[hp], qt[hp * dh:(hp + 1) * dh, :].astype(bf16),
                        preferred_element_type=f32)
            for c in range(tb // LANE):
                _top_values(s[:, c * LANE:(c + 1) * LANE], top_ref, hp, slice(c * LANE, (c + 1) * LANE))
            top_ref[hp, NTOP:, :] = jnp.full((TOPPAD - NTOP, tb), neg, f32)
            smax = top_ref[hp, 0:1, :]
            ex = jnp.exp(s - smax)
            if half == 0:
                e1_ref[head] = ex
            else:
                e2_ref[head] = ex.astype(bf16)
            top_ref[hp] = jnp.exp(top_ref[hp] - smax)
        for head in range(nheads):
            ta, tb_ = 2 * head, 2 * head + 1
            cand_ref[0:TOPPAD, :] = top_ref[ta, 0:1, :] * top_ref[tb_]
            for k1 in range(1, CAND_SUB):
                r0 = TOPPAD + (k1 - 1) * CAND_SUB
                cand_ref[r0:r0 + CAND_SUB, :] = top_ref[ta, k1:k1 + 1, :] * top_ref[tb_, 0:CAND_SUB, :]
            r0 = TOPPAD + (CAND_SUB - 1) * CAND_SUB
            cand_ref[r0:, :] = top_ref[ta, CAND_SUB:, :] * top_ref[tb_, 0:1, :]
            tops = []
            for _ in range(NTOP):
                w = cand_ref[...]
                m = jnp.max(w, axis=0, keepdims=True)
                tops.append(m)
                cand_ref[...] = jnp.where(w == m, -1.0, w)
            zsum = tops[0]
            for k in range(1, PEER_TOPK):
                zsum = zsum + jnp.maximum(tops[k], 0.0)
            inv = 0.5 / zsum
            e1_ref[head] = e1_ref[head] * inv
            mid = 0.5 * (jnp.maximum(tops[PEER_TOPK - 1], 0.0) + jnp.maximum(tops[PEER_TOPK], 0.0))
            th_ref[head:head + 1, :] = mid * inv
        acc_ref[...] = jnp.zeros(acc_ref.shape, f32)

    ipb = eb // nkeys
    i0 = pl.multiple_of(e * ipb, ipb)
    nchunk = tb // LANE
    npiece = eb // MXU_K
    rows_per_piece = MXU_K // nkeys

    def routing_weights(ii, c):
        cols = slice(c * LANE, (c + 1) * LANE)
        w = None
        for head in range(nheads):
            e1row = e1_ref[head, pl.ds(i0, ipb), cols][ii:ii + 1, :].astype(bf16)
            p = e1row * e2_ref[head, :, cols]
            kept = jnp.where(p >= th_ref[head:head + 1, cols].astype(bf16), p, jnp.zeros_like(p))
            w = kept if w is None else w + kept
        return w

    act_ref[...] = jnp.dot(u_ref[...], hnt_ref[...], preferred_element_type=f32)
    for k in range(npiece):
        ks = slice(k * MXU_K, (k + 1) * MXU_K)
        for ii in range(k * rows_per_piece, (k + 1) * rows_per_piece):
            rows = slice(ii * nkeys, (ii + 1) * nkeys)
            for c in range(nchunk):
                cols = slice(c * LANE, (c + 1) * LANE)
                w = routing_weights(ii, c)
                wa_ref[rows, cols] = w * _gelu_x2(act_ref[rows, cols].astype(bf16))
        acc_ref[...] += jnp.dot(vt_ref[0, :, ks], wa_ref[ks, :], preferred_element_type=f32)

    @pl.when(e == ne - 1)
    def _finish():
        o_ref[...] = h_ref[...] + acc_ref[...].T


def _peer(h2d, g, wq_t, keys, u_tab, v_tab, *, nheads):
    t, d = h2d.shape
    nexp = u_tab.shape[0]
    nkeys = keys.shape[1]
    tb = _pick(t, (512, 384, 128))
    eb = 16 * nkeys
    qrows = wq_t.shape[0]
    ne = nexp // eb
    vt_tab = v_tab.reshape(ne, eb, d).transpose(0, 2, 1)
    tok = lambda i, e: (i, 0)
    const2 = lambda i, e: (0, 0)
    return pl.pallas_call(
        functools.partial(_peer_kernel, tb=tb, eb=eb, nheads=nheads, nkeys=nkeys, ne=ne),
        grid=(t // tb, ne),
        in_specs=[pl.BlockSpec((tb, d), tok),
                  pl.BlockSpec((1, d), const2),
                  pl.BlockSpec((qrows, d), const2),
                  pl.BlockSpec(keys.shape, lambda i, e: (0, 0, 0)),
                  pl.BlockSpec((eb, d), lambda i, e: (e, 0)),
                  pl.BlockSpec((1, d, eb), lambda i, e: (e, 0, 0))],
        out_specs=pl.BlockSpec((tb, d), tok),
        out_shape=jax.ShapeDtypeStruct((t, d), f32),
        scratch_shapes=[pltpu.VMEM((d, tb), bf16),
                        pltpu.VMEM((nheads, nkeys, tb), f32),
                        pltpu.VMEM((nheads, nkeys, tb), bf16),
                        pltpu.VMEM((nheads, tb), f32),
                        pltpu.VMEM((2 * nheads, TOPPAD, tb), f32),
                        pltpu.VMEM((CAND_ROWS, tb), f32),
                        pltpu.VMEM((d, tb), f32),
                        pltpu.VMEM((eb, tb), f32),
                        pltpu.VMEM((eb, tb), bf16)],
        compiler_params=_cparams(("parallel", "arbitrary")),
        name="peer",
    )(h2d, g, wq_t, keys, u_tab, vt_tab)


def _final_kernel(h_ref, g_ref, o_ref):
    x = h_ref[0]
    ms = jnp.mean(x * x, axis=-1, keepdims=True)
    o_ref[0] = x * lax.rsqrt(ms + NORM_EPS) * g_ref[...]


def _final(h3d, g, *, b0, nbatch, s):
    _, lp, d = h3d.shape
    tr = LANE
    return pl.pallas_call(
        _final_kernel,
        grid=(nbatch, s // tr),
        in_specs=[pl.BlockSpec((1, tr, d), lambda b, r: (b0 + b, r + 1, 0)),
                  pl.BlockSpec((1, d), lambda b, r: (0, 0))],
        out_specs=pl.BlockSpec((1, tr, d), lambda b, r: (b, r, 0)),
        out_shape=jax.ShapeDtypeStruct((nbatch, s, d), f32),
        compiler_params=_cparams(("parallel", "parallel")),
        name="final_norm",
    )(h3d, g)


def _block_diag(n):
    a = jnp.arange(n) // HEAD_DIM
    return (a[:, None] == a[None, :]).astype(bf16)


def _rope_tables(s, lp, gq, gk, n_heads, n_kv):
    rows = s // GRID_W
    row = jnp.repeat(jnp.arange(rows, dtype=f32), GRID_W)
    col = jnp.tile(jnp.arange(GRID_W, dtype=f32), rows)
    per_axis = HEAD_DIM // 4
    inv = ROPE_THETA ** (-jnp.arange(per_axis, dtype=f32) / per_axis)
    ang = jnp.concatenate([row[:, None] * inv, col[:, None] * inv], axis=-1)
    ang = jnp.concatenate([jnp.zeros((lp - s, HEAD_DIM // 2), f32), ang], axis=0)
    cos, sin = jnp.cos(ang), jnp.sin(ang)
    cfull = jnp.repeat(cos, 2, axis=1)
    sfull = jnp.stack([-sin, sin], axis=-1).reshape(lp, HEAD_DIM)
    swap = jnp.arange(HEAD_DIM) ^ 1
    scale = HEAD_DIM ** -0.5 * math.log2(math.e)
    cq, sq = cfull * gq * scale, sfull * gq[swap] * scale
    ck, sk = cfull * gk, sfull * gk[swap]
    ctab = jnp.concatenate([jnp.tile(cq, (1, n_heads)), jnp.tile(ck, (1, n_kv))], axis=1)
    stab = jnp.concatenate([jnp.tile(sq, (1, n_heads)), jnp.tile(sk, (1, n_kv))], axis=1)
    return ctab, stab


def kernel(x_prompt, x_sample, meta_tokens, norm1_g, w_in, q_norm_g, k_norm_g, hy_conv_w, hy_conv_b, hy_ffn_w1, hy_ffn_b1, hy_sin_freq, hy_ffn_w2, hy_ffn_b2, hy_ffn_w3, hy_decay, hy_dskip, attn_out_g, hy_out_g, w_out, norm2_g, peer_wq, peer_keys, peer_u, peer_v, final_g):
    depth, d, _ = w_in.shape
    b_p, s, _ = x_prompt.shape
    b_s = x_sample.shape[0]
    assert x_sample.shape[1] == s and s % LANE == 0 and s % GRID_W == 0
    nb = b_p + b_s
    seq_l = s + N_META
    lp = s + LANE
    nj = lp // LANE
    t = nb * lp
    n_heads = attn_out_g.shape[1]
    aw = n_heads * HEAD_DIM
    hw = d - aw
    kw = (w_in.shape[2] - aw - 3 * hw) // 2
    n_kv = kw // HEAD_DIM
    assert kw == LANE, "k/v lane-pair layout assumes two kv heads"
    peer_heads = peer_keys.shape[2]
    nkeys = peer_keys.shape[3]
    assert nkeys == LANE

    x = jnp.concatenate([x_prompt, x_sample], axis=0)
    front = jnp.concatenate([jnp.zeros((PADF, d), f32), meta_tokens.astype(f32)], axis=0)
    h = jnp.concatenate([jnp.broadcast_to(front[None], (nb, LANE, d)), x], axis=1).reshape(t, d)

    bd_qk = _block_diag(aw + kw)
    bd_hy = _block_diag(hw)
    swap_q = jnp.arange(aw) ^ 1
    swap_k = jnp.arange(kw) ^ 1
    bands = jnp.linspace(1e-4, HY_BANDS - 1, HY_BANDS, dtype=f32)
    bands_p = jnp.zeros((1, LANE), f32).at[0, :HY_BANDS].set(bands).at[0, HY_BANDS:2 * HY_BANDS].set(bands)

    for l in range(depth):
        w = w_in[l]
        wq, wk = w[:, :aw], w[:, aw:aw + kw]
        wv, wz = w[:, aw + kw:aw + 2 * kw], w[:, aw + 2 * kw:]
        w_ext = jnp.concatenate([wq, wk, wq[:, swap_q], wk[:, swap_k], wv, wz], axis=1).astype(bf16)
        ctab, stab = _rope_tables(s, lp, q_norm_g[l], k_norm_g[l], n_heads, n_kv)
        q, kx, vx, z = _inproj(h, norm1_g[l][None, :], w_ext, bd_qk, ctab, stab,
                               nb=nb, lp=lp, qw=aw, kw=kw, zw=3 * hw)
        o_att = _attention(q, kx, vx, attn_out_g[l].reshape(1, aw), nb=nb, lp=lp, n_kv=n_kv)

        w1 = hy_ffn_w1[l]
        fh = w1.shape[1]
        w1p = jnp.zeros((LANE, fh), f32).at[:2 * HY_BANDS].set(w1[1:]).at[2 * HY_BANDS].set(w1[0])
        filt = _hyfilter(bands_p, w1p, hy_ffn_b1[l][None], hy_sin_freq[l], hy_ffn_w2[l], hy_ffn_b2[l][None],
                         hy_ffn_w3[l], hy_decay[l].reshape(1, 2 * hw), lp=lp, seq_l=seq_l)
        hf, hb = filt[:seq_l, :hw], filt[:seq_l, hw:]
        lags = jnp.concatenate([jnp.zeros((lp - seq_l + 1, hw), f32), hb[:0:-1], hf,
                                jnp.zeros((lp - seq_l, hw), f32)], axis=0)
        hc = lags.T.reshape(hw, 2 * nj, LANE)

        u, x1 = _hypre(z, hy_conv_w[l], hy_conv_b[l][None], nb=nb, lp=lp, hw=hw)
        ut = u.reshape(nb, nj, LANE, hw).transpose(3, 1, 0, 2).reshape(hw, nj * nb, LANE)
        yt = _hyconv(ut, hc, nj=nj, nb=nb)
        y = yt.reshape(hw, nj, nb, LANE).transpose(2, 1, 3, 0).reshape(t, hw)

        h = _outproj(h, o_att, y, u, x1, hy_dskip[l][None], hy_out_g[l].reshape(1, hw), bd_hy,
                     w_out[l].astype(bf16))

        dh = peer_keys.shape[4]
        keys = peer_keys[l].transpose(1, 0, 2, 3).reshape(2 * peer_heads, nkeys, dh).astype(bf16)
        h = _peer(h, norm2_g[l][None], peer_wq[l].T.astype(bf16), keys,
                  peer_u[l].astype(bf16), peer_v[l].astype(bf16), nheads=peer_heads)

    h3 = h.reshape(nb, lp, d)
    y_prompt = _final(h3, final_g[None], b0=0, nbatch=b_p, s=s)
    y_sample = _final(h3, final_g[None], b0=b_p, nbatch=b_s, s=s)
    return (y_prompt, y_sample)
```

```python
import functools
import math

import jax
import jax.numpy as jnp
from jax import lax
from jax.experimental import pallas as pl
from jax.experimental.pallas import tpu as pltpu

N_META = 16
GRID_W = 64
HEAD_DIM = 64
ROPE_THETA = 10000.0
HY_BANDS = 16
PEER_TOPK = 16
NORM_EPS = 1e-6
LANE = 128
MXU_K = 256
PADF = LANE - N_META
VMEM_LIMIT = 56 * 1024 * 1024

f32 = jnp.float32
bf16 = jnp.bfloat16


def _cparams(sem):
    return pltpu.CompilerParams(dimension_semantics=sem, vmem_limit_bytes=VMEM_LIMIT)


def _pick(n, cands):
    for c in cands:
        if n % c == 0:
            return c
    raise ValueError(f"no block size for {n}")


def _inproj_kernel(x_ref, g_ref, w_ref, bd_ref, c_ref, s_ref,
                   q_ref, k_ref, v_ref, z_ref, *, qw, kw, tr):
    r = pl.program_id(1)
    x = x_ref[...]
    ms = jnp.mean(x * x, axis=-1, keepdims=True)
    hn = (x * lax.rsqrt(ms + NORM_EPS) * g_ref[...]).astype(bf16)
    proj = jnp.dot(hn, w_ref[...], preferred_element_type=f32)
    qkw = qw + kw
    qk = proj[:, :qkw]
    qk_swapped = proj[:, qkw:2 * qkw]
    msq = jnp.dot((qk * qk).astype(bf16), bd_ref[...], preferred_element_type=f32) * (1.0 / HEAD_DIM)
    qkr = lax.rsqrt(msq + NORM_EPS) * (qk * c_ref[...] + qk_swapped * s_ref[...])
    q_ref[...] = qkr[:, :qw].astype(bf16)

    lane = lax.broadcasted_iota(jnp.int32, (tr, LANE), 1)
    lo = lane < HEAD_DIM

    def ext(a, fill_lo, fill_hi):
        ar = pltpu.roll(a, HEAD_DIM, axis=1)
        return jnp.concatenate([jnp.where(lo, a, fill_hi), jnp.where(lo, fill_lo, ar),
                                jnp.where(lo, ar, fill_hi), jnp.where(lo, fill_lo, a)], axis=1)

    zero = jnp.zeros((tr, LANE), f32)
    k_ref[...] = ext(qkr[:, qw:qkw], zero, zero).astype(bf16)
    ones_lo = jnp.where(lane == 0, 1.0, 0.0)
    ones_hi = jnp.where(lane == HEAD_DIM, 1.0, 0.0)
    v_ref[...] = ext(proj[:, 2 * qkw:2 * qkw + kw], ones_lo, ones_hi).astype(bf16)
    z = proj[:, 2 * qkw + kw:]
    pos = r * tr + lax.broadcasted_iota(jnp.int32, (tr, 1), 0)
    z_ref[...] = jnp.where(pos >= PADF, z, 0.0).astype(bf16)


def _inproj(h2d, g, w_ext, bd, ctab, stab, *, nb, lp, qw, kw, zw):
    d = h2d.shape[1]
    tr = _pick(lp, (384, 128))
    nr = lp // tr
    t = nb * lp
    wcols = w_ext.shape[1]
    qkw = qw + kw
    row = lambda b, r: (b * nr + r, 0)
    const = lambda b, r: (0, 0)
    return pl.pallas_call(
        functools.partial(_inproj_kernel, qw=qw, kw=kw, tr=tr),
        grid=(nb, nr),
        in_specs=[pl.BlockSpec((tr, d), row),
                  pl.BlockSpec((1, d), const),
                  pl.BlockSpec((d, wcols), const),
                  pl.BlockSpec((qkw, qkw), const),
                  pl.BlockSpec((tr, qkw), lambda b, r: (r, 0)),
                  pl.BlockSpec((tr, qkw), lambda b, r: (r, 0))],
        out_specs=[pl.BlockSpec((tr, qw), row),
                   pl.BlockSpec((tr, 4 * kw), row),
                   pl.BlockSpec((tr, 4 * kw), row),
                   pl.BlockSpec((tr, zw), row)],
        out_shape=[jax.ShapeDtypeStruct((t, qw), bf16),
                   jax.ShapeDtypeStruct((t, 4 * kw), bf16),
                   jax.ShapeDtypeStruct((t, 4 * kw), bf16),
                   jax.ShapeDtypeStruct((t, zw), bf16)],
        compiler_params=_cparams(("parallel", "parallel")),
        name="inproj",
    )(h2d, g, w_ext, bd, ctab, stab)


SM_ROWS = 16


def _attn_kernel(q_ref, k_ref, v_ref, g_ref, o_ref, s_a, s_b, p_a, p_b, *, tq):
    lane = lax.broadcasted_iota(jnp.int32, (tq, LANE), 1)
    lo = lane < HEAD_DIM
    real = lax.broadcasted_iota(jnp.int32, (SM_ROWS, LANE), 1) >= PADF
    nt = (((1,), (1,)), ((), ()))
    heads = [(pr, hh) for pr in range(q_ref.shape[1] // LANE) for hh in range(2)]
    s_bufs, p_bufs = (s_a, s_b), (p_a, p_b)

    def scores(n):
        pr, hh = heads[n]
        s_bufs[n % 2][...] = lax.dot_general(q_ref[:, pr * LANE:(pr + 1) * LANE],
                                             k_ref[:, hh * LANE:(hh + 1) * LANE], nt,
                                             preferred_element_type=f32)

    def softmax_rows(s_ref, p_ref, r):
        rows = slice(r * SM_ROWS, (r + 1) * SM_ROWS)
        s0 = jnp.where(real, s_ref[rows, :LANE], -jnp.inf)
        s1 = s_ref[rows, LANE:]
        m = jnp.maximum(jnp.max(s0, axis=-1, keepdims=True), jnp.max(s1, axis=-1, keepdims=True))
        p_ref[rows, :LANE] = jnp.exp2((s0 - m).astype(bf16))
        p_ref[rows, LANE:] = jnp.exp2((s1 - m).astype(bf16))

    scores(0)
    for n, (pr, hh) in enumerate(heads):
        cols = slice(pr * LANE, (pr + 1) * LANE)
        if n + 1 < len(heads):
            scores(n + 1)
        for r in range(tq // SM_ROWS):
            softmax_rows(s_bufs[n % 2], p_bufs[n % 2], r)
        oh = jnp.dot(p_bufs[n % 2][...], v_ref[:, hh * LANE:(hh + 1) * LANE], preferred_element_type=f32)
        sum_lane = HEAD_DIM if hh == 0 else 0
        oh = oh * (1.0 / oh[:, sum_lane:sum_lane + 1])
        if hh == 0:
            o = oh
            continue
        o = jnp.where(lo, o, oh)
        o2 = o * o
        s_all = jnp.sum(o2, axis=-1, keepdims=True)
        s_lo = jnp.sum(jnp.where(lo, o2, 0.0), axis=-1, keepdims=True)
        ms = jnp.where(lo, s_lo, s_all - s_lo) * (1.0 / HEAD_DIM)
        o_ref[:, cols] = (o * lax.rsqrt(ms + NORM_EPS) * g_ref[:, cols]).astype(bf16)


def _attention(q, kx, vx, g_out, *, nb, lp, n_kv):
    t, qw = q.shape
    gw = qw // n_kv
    assert gw % LANE == 0
    tq = _pick(lp, (384, 128))
    nq = lp // tq
    return pl.pallas_call(
        functools.partial(_attn_kernel, tq=tq),
        grid=(nb, n_kv, nq),
        in_specs=[pl.BlockSpec((tq, gw), lambda b, h, i: (b * nq + i, h)),
                  pl.BlockSpec((lp, 2 * LANE), lambda b, h, i: (b, h)),
                  pl.BlockSpec((lp, 2 * LANE), lambda b, h, i: (b, h)),
                  pl.BlockSpec((1, gw), lambda b, h, i: (0, h))],
        out_specs=pl.BlockSpec((tq, gw), lambda b, h, i: (b * nq + i, h)),
        out_shape=jax.ShapeDtypeStruct((t, qw), bf16),
        scratch_shapes=[pltpu.VMEM((tq, lp), f32), pltpu.VMEM((tq, lp), f32),
                        pltpu.VMEM((tq, lp), bf16), pltpu.VMEM((tq, lp), bf16)],
        compiler_params=_cparams(("parallel", "parallel", "parallel")),
        name="attention",
    )(q, kx, vx, g_out)


def _hyfilter_kernel(bands_ref, w1_ref, b1_ref, sf_ref, w2_ref, b2_ref, w3_ref, dec_ref, o_ref,
                     *, tr, seq_l):
    r = pl.program_id(0)
    hp = lax.Precision.HIGHEST
    t = (r * tr + lax.broadcasted_iota(jnp.int32, (tr, 1), 0)).astype(f32)
    tn = t / (seq_l - 1)
    ang = ((2.0 * math.pi / seq_l) * t) * bands_ref[...]
    lane = lax.broadcasted_iota(jnp.int32, (tr, LANE), 1)
    feat = jnp.where(lane < HY_BANDS, jnp.cos(ang),
                     jnp.where(lane < 2 * HY_BANDS, jnp.sin(ang),
                               jnp.where(lane == 2 * HY_BANDS, tn, 0.0)))
    sf = sf_ref[...]
    h = jnp.sin(sf[0:1] * (jnp.dot(feat, w1_ref[...], precision=hp, preferred_element_type=f32) + b1_ref[...]))
    h = jnp.sin(sf[1:2] * (jnp.dot(h, w2_ref[...], precision=hp, preferred_element_type=f32) + b2_ref[...]))
    h = jnp.dot(h, w3_ref[...], precision=hp, preferred_element_type=f32)
    o_ref[...] = h * jnp.exp(-tn * dec_ref[...])


def _hyfilter(bands, w1p, b1, sf, w2, b2, w3, dec, *, lp, seq_l):
    tr = _pick(lp, (384, 128))
    fh = w2.shape[0]
    wout = w3.shape[1]
    const = lambda r: (0, 0)
    return pl.pallas_call(
        functools.partial(_hyfilter_kernel, tr=tr, seq_l=seq_l),
        grid=(lp // tr,),
        in_specs=[pl.BlockSpec((1, LANE), const), pl.BlockSpec((LANE, fh), const),
                  pl.BlockSpec((1, fh), const), pl.BlockSpec((2, fh), const),
                  pl.BlockSpec((fh, fh), const), pl.BlockSpec((1, fh), const),
                  pl.BlockSpec((fh, wout), const), pl.BlockSpec((1, wout), const)],
        out_specs=pl.BlockSpec((tr, wout), lambda r: (r, 0)),
        out_shape=jax.ShapeDtypeStruct((lp, wout), f32),
        compiler_params=_cparams(("parallel",)),
        name="hyena_filter",
    )(bands, w1p, b1, sf, w2, b2, w3, dec)


HALO = 16


def _hypre_kernel(z_ref, zp_ref, zn_ref, cw_ref, cb_ref, u_ref, x1_ref, *, tr, nr, hw):
    r = pl.program_id(1)
    z = z_ref[...].astype(f32)
    prev_row = jnp.where(r > 0, zp_ref[HALO - 1:HALO, :].astype(f32), 0.0)
    next_row = jnp.where(r < nr - 1, zn_ref[0:1, :].astype(f32), 0.0)
    ridx = lax.broadcasted_iota(jnp.int32, (tr, 1), 0)
    z_prev = jnp.where(ridx == 0, prev_row, pltpu.roll(z, 1, axis=0))
    z_next = jnp.where(ridx == tr - 1, next_row, pltpu.roll(z, tr - 1, axis=0))
    cw = cw_ref[...]
    zc = z_prev * cw[0:1] + z * cw[1:2] + z_next * cw[2:3] + cb_ref[...]
    x0 = zc[:, :hw]
    x1 = zc[:, hw:2 * hw]
    vv = zc[:, 2 * hw:]
    pos = r * tr + ridx
    u_ref[...] = jnp.where(pos >= PADF, vv * x0, 0.0)
    x1_ref[...] = x1


def _hypre(z, cw, cb, *, nb, lp, hw):
    t, zw = z.shape
    tr = _pick(lp, (384, 128))
    nr = lp // tr
    per = tr // HALO
    nh = t // HALO
    row = lambda b, r: (b * nr + r, 0)
    const = lambda b, r: (0, 0)
    return pl.pallas_call(
        functools.partial(_hypre_kernel, tr=tr, nr=nr, hw=hw),
        grid=(nb, nr),
        in_specs=[pl.BlockSpec((tr, zw), row),
                  pl.BlockSpec((HALO, zw), lambda b, r: (jnp.maximum((b * nr + r) * per - 1, 0), 0)),
                  pl.BlockSpec((HALO, zw), lambda b, r: (jnp.minimum((b * nr + r + 1) * per, nh - 1), 0)),
                  pl.BlockSpec((3, zw), const), pl.BlockSpec((1, zw), const)],
        out_specs=[pl.BlockSpec((tr, hw), row), pl.BlockSpec((tr, hw), row)],
        out_shape=[jax.ShapeDtypeStruct((t, hw), f32), jax.ShapeDtypeStruct((t, hw), f32)],
        compiler_params=_cparams(("parallel", "parallel")),
        name="hyena_shortconv",
    )(z, z, z, cw, cb)


def _hyconv_kernel(u_ref, h_ref, y_ref, rl_ref, *, nj, nb):
    jj = lax.broadcasted_iota(jnp.int32, (LANE, LANE), 0)
    ii = lax.broadcasted_iota(jnp.int32, (LANE, LANE), 1)
    upper = ii >= jj
    y_ref[...] = jnp.zeros(y_ref.shape, f32)

    def rolled(r):
        row = jnp.broadcast_to(h_ref[0, r:r + 1, :], (LANE, LANE))
        return pltpu.roll(row, 0, axis=1, stride=1, stride_axis=0)

    rl_ref[...] = rolled(0)
    for d in range(-(nj - 1), nj):
        hi = rolled(d + nj)
        tile = jnp.where(upper, hi, rl_ref[...]).astype(bf16)
        rl_ref[...] = hi
        j0, j1 = max(0, -d), min(nj, nj - d)
        src = u_ref[0, j0 * nb:j1 * nb, :].astype(bf16)
        dst = slice((j0 + d) * nb, (j1 + d) * nb)
        y_ref[0, dst, :] += jnp.dot(src, tile, preferred_element_type=f32)


def _hyconv(ut, hc, *, nj, nb):
    c, rows, _ = ut.shape
    return pl.pallas_call(
        functools.partial(_hyconv_kernel, nj=nj, nb=nb),
        grid=(c,),
        in_specs=[pl.BlockSpec((1, rows, LANE), lambda i: (i, 0, 0)),
                  pl.BlockSpec((1, 2 * nj, LANE), lambda i: (i, 0, 0))],
        out_specs=pl.BlockSpec((1, rows, LANE), lambda i: (i, 0, 0)),
        out_shape=jax.ShapeDtypeStruct((c, rows, LANE), f32),
        scratch_shapes=[pltpu.VMEM((LANE, LANE), f32)],
        compiler_params=_cparams(("parallel",)),
        name="hyena_longconv",
    )(ut, hc)


def _outproj_kernel(h_ref, oa_ref, y_ref, u_ref, x1_ref, ds_ref, gh_ref, bd_ref, w_ref, o_ref):
    ohy = (y_ref[...] + u_ref[...] * ds_ref[...]) * x1_ref[...]
    ms = jnp.dot((ohy * ohy).astype(bf16), bd_ref[...], preferred_element_type=f32) * (1.0 / HEAD_DIM)
    ohn = (ohy * lax.rsqrt(ms + NORM_EPS) * gh_ref[...]).astype(bf16)
    mixed = jnp.concatenate([oa_ref[...], ohn], axis=1)
    o_ref[...] = h_ref[...] + jnp.dot(mixed, w_ref[...], preferred_element_type=f32)


def _outproj(h2d, oa, y, u, x1, dskip, gh, bd, w_out):
    t, d = h2d.shape
    aw = oa.shape[1]
    hw = y.shape[1]
    tr = _pick(t, (384, 128))
    row = lambda r: (r, 0)
    const = lambda r: (0, 0)
    return pl.pallas_call(
        _outproj_kernel,
        grid=(t // tr,),
        in_specs=[pl.BlockSpec((tr, d), row), pl.BlockSpec((tr, aw), row),
                  pl.BlockSpec((tr, hw), row), pl.BlockSpec((tr, hw), row), pl.BlockSpec((tr, hw), row),
                  pl.BlockSpec((1, hw), const), pl.BlockSpec((1, hw), const),
                  pl.BlockSpec((hw, hw), const), pl.BlockSpec((d, d), const)],
        out_specs=pl.BlockSpec((tr, d), row),
        out_shape=jax.ShapeDtypeStruct((t, d), f32),
        compiler_params=_cparams(("parallel",)),
        name="outproj",
    )(h2d, oa, y, u, x1, dskip, gh, bd, w_out)


NTOP = PEER_TOPK + 1
TOPPAD = 24


def _gelu_x2(x):
    return x * (1.0 + lax.erf(x * (1.0 / math.sqrt(2.0))))


SUBLANES = 8


def _sort_network(n):
    def merge(lo, hi, r):
        step = r * 2
        if step < hi - lo:
            yield from merge(lo, hi, step)
            yield from merge(lo + r, hi, step)
            for i in range(lo + r, hi - r, step):
                yield (i, i + r)
        else:
            yield (lo, lo + r)

    def sort(lo, hi):
        if hi - lo >= 1:
            mid = lo + (hi - lo) // 2
            yield from sort(lo, mid)
            yield from sort(mid + 1, hi)
            yield from merge(lo, hi, 1)

    return list(sort(0, n - 1))


def _top_values(s, top_ref, hp, cols):
    nt = s.shape[0] // SUBLANES
    v = [s[i * SUBLANES:(i + 1) * SUBLANES, :] for i in range(nt)]
    for i, j in _sort_network(nt):
        v[i], v[j] = jnp.maximum(v[i], v[j]), jnp.minimum(v[i], v[j])
    v.append(jnp.full((SUBLANES, LANE), -jnp.inf, f32))
    for t in range(NTOP):
        m = jnp.max(v[0], axis=0, keepdims=True)
        top_ref[hp, t:t + 1, cols] = m
        pop = v[0] == m
        for k in range(min(nt, NTOP - 1 - t)):
            v[k] = jnp.where(pop, v[k + 1], v[k])


CAND_SUB = 8
CAND_ROWS = TOPPAD + (CAND_SUB - 1) * CAND_SUB + 2 * CAND_SUB


def _peer_kernel(h_ref, g_ref, wq_ref, keys_ref, u_ref, vt_ref, o_ref,
                 hnt_ref, e1_ref, e2_ref, th_ref, top_ref, cand_ref, acc_ref,
                 act_ref, wa_ref, *, tb, eb, nheads, nkeys, ne):
    e = pl.program_id(1)
    neg = -jnp.inf

    @pl.when(e == 0)
    def _route():
        x = h_ref[...]
        ms = jnp.mean(x * x, axis=-1, keepdims=True)
        hn = x * lax.rsqrt(ms + NORM_EPS) * g_ref[...]
        hnt = hn.T.astype(bf16)
        hnt_ref[...] = hnt
        qt = jnp.dot(wq_ref[...], hnt, preferred_element_type=f32)
        dh = keys_ref.shape[2]
        for hp in range(2 * nheads):
            head, half = divmod(hp, 2)
            s = jnp.dot(keys_ref[hp], qt[hp * dh:(hp + 1) * dh, :].astype(bf16),
                        preferred_element_type=f32)
            for c in range(tb // LANE):
                _top_values(s[:, c * LANE:(c + 1) * LANE], top_ref, hp, slice(c * LANE, (c + 1) * LANE))
            top_ref[hp, NTOP:, :] = jnp.full((TOPPAD - NTOP, tb), neg, f32)
            smax = top_ref[hp, 0:1, :]
            ex = jnp.exp(s - smax)
            if half == 0:
                e1_ref[head] = ex
            else:
                e2_ref[head] = ex.astype(bf16)
            top_ref[hp] = jnp.exp(top_ref[hp] - smax)
        for head in range(nheads):
            ta, tb_ = 2 * head, 2 * head + 1
            cand_ref[0:TOPPAD, :] = top_ref[ta, 0:1, :] * top_ref[tb_]
            for k1 in range(1, CAND_SUB):
                r0 = TOPPAD + (k1 - 1) * CAND_SUB
                cand_ref[r0:r0 + CAND_SUB, :] = top_ref[ta, k1:k1 + 1, :] * top_ref[tb_, 0:CAND_SUB, :]
            r0 = TOPPAD + (CAND_SUB - 1) * CAND_SUB
            cand_ref[r0:, :] = top_ref[ta, CAND_SUB:, :] * top_ref[tb_, 0:1, :]
            tops = []
            for _ in range(NTOP):
                w = cand_ref[...]
                m = jnp.max(w, axis=0, keepdims=True)
                tops.append(m)
                cand_ref[...] = jnp.where(w == m, -1.0, w)
            zsum = tops[0]
            for k in range(1, PEER_TOPK):
                zsum = zsum + jnp.maximum(tops[k], 0.0)
            inv = 0.5 / zsum
            e1_ref[head] = e1_ref[head] * inv
            mid = 0.5 * (jnp.maximum(tops[PEER_TOPK - 1], 0.0) + jnp.maximum(tops[PEER_TOPK], 0.0))
            th_ref[head:head + 1, :] = mid * inv
        acc_ref[...] = jnp.zeros(acc_ref.shape, f32)

    ipb = eb // nkeys
    i0 = pl.multiple_of(e * ipb, ipb)
    nchunk = tb // LANE
    npiece = eb // MXU_K
    rows_per_piece = MXU_K // nkeys

    def routing_weights(ii, c):
        cols = slice(c * LANE, (c + 1) * LANE)
        w = None
        for head in range(nheads):
            e1row = e1_ref[head, pl.ds(i0, ipb), cols][ii:ii + 1, :].astype(bf16)
            p = e1row * e2_ref[head, :, cols]
            kept = jnp.where(p >= th_ref[head:head + 1, cols].astype(bf16), p, jnp.zeros_like(p))
            w = kept if w is None else w + kept
        return w

    act_ref[...] = jnp.dot(u_ref[...], hnt_ref[...], preferred_element_type=f32)
    for k in range(npiece):
        ks = slice(k * MXU_K, (k + 1) * MXU_K)
        for ii in range(k * rows_per_piece, (k + 1) * rows_per_piece):
            rows = slice(ii * nkeys, (ii + 1) * nkeys)
            for c in range(nchunk):
                cols = slice(c * LANE, (c + 1) * LANE)
                w = routing_weights(ii, c)
                wa_ref[rows, cols] = w * _gelu_x2(act_ref[rows, cols].astype(bf16))
        acc_ref[...] += jnp.dot(vt_ref[0, :, ks], wa_ref[ks, :], preferred_element_type=f32)

    @pl.when(e == ne - 1)
    def _finish():
        o_ref[...] = h_ref[...] + acc_ref[...].T


def _peer(h2d, g, wq_t, keys, u_tab, v_tab, *, nheads):
    t, d = h2d.shape
    nexp = u_tab.shape[0]
    nkeys = keys.shape[1]
    tb = _pick(t, (512, 384, 128))
    eb = 16 * nkeys
    qrows = wq_t.shape[0]
    ne = nexp // eb
    vt_tab = v_tab.reshape(ne, eb, d).transpose(0, 2, 1)
    tok = lambda i, e: (i, 0)
    const2 = lambda i, e: (0, 0)
    return pl.pallas_call(
        functools.partial(_peer_kernel, tb=tb, eb=eb, nheads=nheads, nkeys=nkeys, ne=ne),
        grid=(t // tb, ne),
        in_specs=[pl.BlockSpec((tb, d), tok),
                  pl.BlockSpec((1, d), const2),
                  pl.BlockSpec((qrows, d), const2),
                  pl.BlockSpec(keys.shape, lambda i, e: (0, 0, 0)),
                  pl.BlockSpec((eb, d), lambda i, e: (e, 0)),
                  pl.BlockSpec((1, d, eb), lambda i, e: (e, 0, 0))],
        out_specs=pl.BlockSpec((tb, d), tok),
        out_shape=jax.ShapeDtypeStruct((t, d), f32),
        scratch_shapes=[pltpu.VMEM((d, tb), bf16),
                        pltpu.VMEM((nheads, nkeys, tb), f32),
                        pltpu.VMEM((nheads, nkeys, tb), bf16),
                        pltpu.VMEM((nheads, tb), f32),
                        pltpu.VMEM((2 * nheads, TOPPAD, tb), f32),
                        pltpu.VMEM((CAND_ROWS, tb), f32),
                        pltpu.VMEM((d, tb), f32),
                        pltpu.VMEM((eb, tb), f32),
                        pltpu.VMEM((eb, tb), bf16)],
        compiler_params=_cparams(("parallel", "arbitrary")),
        name="peer",
    )(h2d, g, wq_t, keys, u_tab, vt_tab)


def _final_kernel(h_ref, g_ref, o_ref):
    x = h_ref[0]
    ms = jnp.mean(x * x, axis=-1, keepdims=True)
    o_ref[0] = x * lax.rsqrt(ms + NORM_EPS) * g_ref[...]


def _final(h3d, g, *, b0, nbatch, s):
    _, lp, d = h3d.shape
    tr = LANE
    return pl.pallas_call(
        _final_kernel,
        grid=(nbatch, s // tr),
        in_specs=[pl.BlockSpec((1, tr, d), lambda b, r: (b0 + b, r + 1, 0)),
                  pl.BlockSpec((1, d), lambda b, r: (0, 0))],
        out_specs=pl.BlockSpec((1, tr, d), lambda b, r: (b, r, 0)),
        out_shape=jax.ShapeDtypeStruct((nbatch, s, d), f32),
        compiler_params=_cparams(("parallel", "parallel")),
        name="final_norm",
    )(h3d, g)


def _block_diag(n):
    a = jnp.arange(n) // HEAD_DIM
    return (a[:, None] == a[None, :]).astype(bf16)


def _rope_tables(s, lp, gq, gk, n_heads, n_kv):
    rows = s // GRID_W
    row = jnp.repeat(jnp.arange(rows, dtype=f32), GRID_W)
    col = jnp.tile(jnp.arange(GRID_W, dtype=f32), rows)
    per_axis = HEAD_DIM // 4
    inv = ROPE_THETA ** (-jnp.arange(per_axis, dtype=f32) / per_axis)
    ang = jnp.concatenate([row[:, None] * inv, col[:, None] * inv], axis=-1)
    ang = jnp.concatenate([jnp.zeros((lp - s, HEAD_DIM // 2), f32), ang], axis=0)
    cos, sin = jnp.cos(ang), jnp.sin(ang)
    cfull = jnp.repeat(cos, 2, axis=1)
    sfull = jnp.stack([-sin, sin], axis=-1).reshape(lp, HEAD_DIM)
    swap = jnp.arange(HEAD_DIM) ^ 1
    scale = HEAD_DIM ** -0.5 * math.log2(math.e)
    cq, sq = cfull * gq * scale, sfull * gq[swap] * scale
    ck, sk = cfull * gk, sfull * gk[swap]
    ctab = jnp.concatenate([jnp.tile(cq, (1, n_heads)), jnp.tile(ck, (1, n_kv))], axis=1)
    stab = jnp.concatenate([jnp.tile(sq, (1, n_heads)), jnp.tile(sk, (1, n_kv))], axis=1)
    return ctab, stab


def kernel(x_prompt, x_sample, meta_tokens, norm1_g, w_in, q_norm_g, k_norm_g, hy_conv_w, hy_conv_b, hy_ffn_w1, hy_ffn_b1, hy_sin_freq, hy_ffn_w2, hy_ffn_b2, hy_ffn_w3, hy_decay, hy_dskip, attn_out_g, hy_out_g, w_out, norm2_g, peer_wq, peer_keys, peer_u, peer_v, final_g):
    depth, d, _ = w_in.shape
    b_p, s, _ = x_prompt.shape
    b_s = x_sample.shape[0]
    assert x_sample.shape[1] == s and s % LANE == 0 and s % GRID_W == 0
    nb = b_p + b_s
    seq_l = s + N_META
    lp = s + LANE
    nj = lp // LANE
    t = nb * lp
    n_heads = attn_out_g.shape[1]
    aw = n_heads * HEAD_DIM
    hw = d - aw
    kw = (w_in.shape[2] - aw - 3 * hw) // 2
    n_kv = kw // HEAD_DIM
    assert kw == LANE, "k/v lane-pair layout assumes two kv heads"
    peer_heads = peer_keys.shape[2]
    nkeys = peer_keys.shape[3]
    assert nkeys == LANE

    x = jnp.concatenate([x_prompt, x_sample], axis=0)
    front = jnp.concatenate([jnp.zeros((PADF, d), f32), meta_tokens.astype(f32)], axis=0)
    h = jnp.concatenate([jnp.broadcast_to(front[None], (nb, LANE, d)), x], axis=1).reshape(t, d)

    bd_qk = _block_diag(aw + kw)
    bd_hy = _block_diag(hw)
    swap_q = jnp.arange(aw) ^ 1
    swap_k = jnp.arange(kw) ^ 1
    bands = jnp.linspace(1e-4, HY_BANDS - 1, HY_BANDS, dtype=f32)
    bands_p = jnp.zeros((1, LANE), f32).at[0, :HY_BANDS].set(bands).at[0, HY_BANDS:2 * HY_BANDS].set(bands)

    for l in range(depth):
        w = w_in[l]
        wq, wk = w[:, :aw], w[:, aw:aw + kw]
        wv, wz = w[:, aw + kw:aw + 2 * kw], w[:, aw + 2 * kw:]
        w_ext = jnp.concatenate([wq, wk, wq[:, swap_q], wk[:, swap_k], wv, wz], axis=1).astype(bf16)
        ctab, stab = _rope_tables(s, lp, q_norm_g[l], k_norm_g[l], n_heads, n_kv)
        q, kx, vx, z = _inproj(h, norm1_g[l][None, :], w_ext, bd_qk, ctab, stab,
                               nb=nb, lp=lp, qw=aw, kw=kw, zw=3 * hw)
        o_att = _attention(q, kx, vx, attn_out_g[l].reshape(1, aw), nb=nb, lp=lp, n_kv=n_kv)

        w1 = hy_ffn_w1[l]
        fh = w1.shape[1]
        w1p = jnp.zeros((LANE, fh), f32).at[:2 * HY_BANDS].set(w1[1:]).at[2 * HY_BANDS].set(w1[0])
        filt = _hyfilter(bands_p, w1p, hy_ffn_b1[l][None], hy_sin_freq[l], hy_ffn_w2[l], hy_ffn_b2[l][None],
                         hy_ffn_w3[l], hy_decay[l].reshape(1, 2 * hw), lp=lp, seq_l=seq_l)
        hf, hb = filt[:seq_l, :hw], filt[:seq_l, hw:]
        lags = jnp.concatenate([jnp.zeros((lp - seq_l + 1, hw), f32), hb[:0:-1], hf,
                                jnp.zeros((lp - seq_l, hw), f32)], axis=0)
        hc = lags.T.reshape(hw, 2 * nj, LANE)

        u, x1 = _hypre(z, hy_conv_w[l], hy_conv_b[l][None], nb=nb, lp=lp, hw=hw)
        ut = u.reshape(nb, nj, LANE, hw).transpose(3, 1, 0, 2).reshape(hw, nj * nb, LANE)
        yt = _hyconv(ut, hc, nj=nj, nb=nb)
        y = yt.reshape(hw, nj, nb, LANE).transpose(2, 1, 3, 0).reshape(t, hw)

        h = _outproj(h, o_att, y, u, x1, hy_dskip[l][None], hy_out_g[l].reshape(1, hw), bd_hy,
                     w_out[l].astype(bf16))

        dh = peer_keys.shape[4]
        keys = peer_keys[l].transpose(1, 0, 2, 3).reshape(2 * peer_heads, nkeys, dh).astype(bf16)
        h = _peer(h, norm2_g[l][None], peer_wq[l].T.astype(bf16), keys,
                  peer_u[l].astype(bf16), peer_v[l].astype(bf16), nheads=peer_heads)

    h3 = h.reshape(nb, lp, d)
    y_prompt = _final(h3, final_g[None], b0=0, nbatch=b_p, s=s)
    y_sample = _final(h3, final_g[None], b0=b_p, nbatch=b_s, s=s)
    return (y_prompt, y_sample)
```

```python
import functools
import math

import jax
import jax.numpy as jnp
from jax import lax
from jax.experimental import pallas as pl
from jax.experimental.pallas import tpu as pltpu

N_META = 16
GRID_W = 64
HEAD_DIM = 64
ROPE_THETA = 10000.0
HY_BANDS = 16
PEER_TOPK = 16
NORM_EPS = 1e-6
LANE = 128
MXU_K = 256
PADF = LANE - N_META
VMEM_LIMIT = 56 * 1024 * 1024

f32 = jnp.float32
bf16 = jnp.bfloat16


def _cparams(sem):
    return pltpu.CompilerParams(dimension_semantics=sem, vmem_limit_bytes=VMEM_LIMIT)


def _pick(n, cands):
    for c in cands:
        if n % c == 0:
            return c
    raise ValueError(f"no block size for {n}")


def _inproj_kernel(x_ref, g_ref, w_ref, bd_ref, c_ref, s_ref,
                   q_ref, k_ref, v_ref, z_ref, *, qw, kw, tr):
    r = pl.program_id(1)
    x = x_ref[...]
    ms = jnp.mean(x * x, axis=-1, keepdims=True)
    hn = (x * lax.rsqrt(ms + NORM_EPS) * g_ref[...]).astype(bf16)
    proj = jnp.dot(hn, w_ref[...], preferred_element_type=f32)
    qkw = qw + kw
    qk = proj[:, :qkw]
    qk_swapped = proj[:, qkw:2 * qkw]
    msq = jnp.dot((qk * qk).astype(bf16), bd_ref[...], preferred_element_type=f32) * (1.0 / HEAD_DIM)
    qkr = lax.rsqrt(msq + NORM_EPS) * (qk * c_ref[...] + qk_swapped * s_ref[...])
    q_ref[...] = qkr[:, :qw].astype(bf16)

    lane = lax.broadcasted_iota(jnp.int32, (tr, LANE), 1)
    lo = lane < HEAD_DIM

    def ext(a, fill_lo, fill_hi):
        ar = pltpu.roll(a, HEAD_DIM, axis=1)
        return jnp.concatenate([jnp.where(lo, a, fill_hi), jnp.where(lo, fill_lo, ar),
                                jnp.where(lo, ar, fill_hi), jnp.where(lo, fill_lo, a)], axis=1)

    zero = jnp.zeros((tr, LANE), f32)
    k_ref[...] = ext(qkr[:, qw:qkw], zero, zero).astype(bf16)
    ones_lo = jnp.where(lane == 0, 1.0, 0.0)
    ones_hi = jnp.where(lane == HEAD_DIM, 1.0, 0.0)
    v_ref[...] = ext(proj[:, 2 * qkw:2 * qkw + kw], ones_lo, ones_hi).astype(bf16)
    z = proj[:, 2 * qkw + kw:]
    pos = r * tr + lax.broadcasted_iota(jnp.int32, (tr, 1), 0)
    z_ref[...] = jnp.where(pos >= PADF, z, 0.0).astype(bf16)


def _inproj(h2d, g, w_ext, bd, ctab, stab, *, nb, lp, qw, kw, zw):
    d = h2d.shape[1]
    tr = _pick(lp, (384, 128))
    nr = lp // tr
    t = nb * lp
    wcols = w_ext.shape[1]
    qkw = qw + kw
    row = lambda b, r: (b * nr + r, 0)
    const = lambda b, r: (0, 0)
    return pl.pallas_call(
        functools.partial(_inproj_kernel, qw=qw, kw=kw, tr=tr),
        grid=(nb, nr),
        in_specs=[pl.BlockSpec((tr, d), row),
                  pl.BlockSpec((1, d), const),
                  pl.BlockSpec((d, wcols), const),
                  pl.BlockSpec((qkw, qkw), const),
                  pl.BlockSpec((tr, qkw), lambda b, r: (r, 0)),
                  pl.BlockSpec((tr, qkw), lambda b, r: (r, 0))],
        out_specs=[pl.BlockSpec((tr, qw), row),
                   pl.BlockSpec((tr, 4 * kw), row),
                   pl.BlockSpec((tr, 4 * kw), row),
                   pl.BlockSpec((tr, zw), row)],
        out_shape=[jax.ShapeDtypeStruct((t, qw), bf16),
                   jax.ShapeDtypeStruct((t, 4 * kw), bf16),
                   jax.ShapeDtypeStruct((t, 4 * kw), bf16),
                   jax.ShapeDtypeStruct((t, zw), bf16)],
        compiler_params=_cparams(("parallel", "parallel")),
        name="inproj",
    )(h2d, g, w_ext, bd, ctab, stab)


SM_ROWS = 16


def _attn_kernel(q_ref, k_ref, v_ref, g_ref, o_ref, s_a, s_b, p_a, p_b, *, tq):
    lane = lax.broadcasted_iota(jnp.int32, (tq, LANE), 1)
    lo = lane < HEAD_DIM
    real = lax.broadcasted_iota(jnp.int32, (SM_ROWS, LANE), 1) >= PADF
    nt = (((1,), (1,)), ((), ()))
    heads = [(pr, hh) for pr in range(q_ref.shape[1] // LANE) for hh in range(2)]
    s_bufs, p_bufs = (s_a, s_b), (p_a, p_b)

    def scores(n):
        pr, hh = heads[n]
        s_bufs[n % 2][...] = lax.dot_general(q_ref[:, pr * LANE:(pr + 1) * LANE],
                                             k_ref[:, hh * LANE:(hh + 1) * LANE], nt,
                                             preferred_element_type=f32)

    def softmax_rows(s_ref, p_ref, r):
        rows = slice(r * SM_ROWS, (r + 1) * SM_ROWS)
        s0 = jnp.where(real, s_ref[rows, :LANE], -jnp.inf)
        s1 = s_ref[rows, LANE:]
        m = jnp.maximum(jnp.max(s0, axis=-1, keepdims=True), jnp.max(s1, axis=-1, keepdims=True))
        p_ref[rows, :LANE] = jnp.exp2(s0 - m).astype(bf16)
        p_ref[rows, LANE:] = jnp.exp2(s1 - m).astype(bf16)

    scores(0)
    for n, (pr, hh) in enumerate(heads):
        cols = slice(pr * LANE, (pr + 1) * LANE)
        if n + 1 < len(heads):
            scores(n + 1)
        for r in range(tq // SM_ROWS):
            softmax_rows(s_bufs[n % 2], p_bufs[n % 2], r)
        oh = jnp.dot(p_bufs[n % 2][...], v_ref[:, hh * LANE:(hh + 1) * LANE], preferred_element_type=f32)
        sum_lane = HEAD_DIM if hh == 0 else 0
        oh = oh * (1.0 / oh[:, sum_lane:sum_lane + 1])
        if hh == 0:
            o = oh
            continue
        o = jnp.where(lo, o, oh)
        o2 = o * o
        s_all = jnp.sum(o2, axis=-1, keepdims=True)
        s_lo = jnp.sum(jnp.where(lo, o2, 0.0), axis=-1, keepdims=True)
        ms = jnp.where(lo, s_lo, s_all - s_lo) * (1.0 / HEAD_DIM)
        o_ref[:, cols] = (o * lax.rsqrt(ms + NORM_EPS) * g_ref[:, cols]).astype(bf16)


def _attention(q, kx, vx, g_out, *, nb, lp, n_kv):
    t, qw = q.shape
    gw = qw // n_kv
    assert gw % LANE == 0
    tq = _pick(lp, (384, 128))
    nq = lp // tq
    return pl.pallas_call(
        functools.partial(_attn_kernel, tq=tq),
        grid=(nb, n_kv, nq),
        in_specs=[pl.BlockSpec((tq, gw), lambda b, h, i: (b * nq + i, h)),
                  pl.BlockSpec((lp, 2 * LANE), lambda b, h, i: (b, h)),
                  pl.BlockSpec((lp, 2 * LANE), lambda b, h, i: (b, h)),
                  pl.BlockSpec((1, gw), lambda b, h, i: (0, h))],
        out_specs=pl.BlockSpec((tq, gw), lambda b, h, i: (b * nq + i, h)),
        out_shape=jax.ShapeDtypeStruct((t, qw), bf16),
        scratch_shapes=[pltpu.VMEM((tq, lp), f32), pltpu.VMEM((tq, lp), f32),
                        pltpu.VMEM((tq, lp), bf16), pltpu.VMEM((tq, lp), bf16)],
        compiler_params=_cparams(("parallel", "parallel", "parallel")),
        name="attention",
    )(q, kx, vx, g_out)


def _hyfilter_kernel(bands_ref, w1_ref, b1_ref, sf_ref, w2_ref, b2_ref, w3_ref, dec_ref, o_ref,
                     *, tr, seq_l):
    r = pl.program_id(0)
    hp = lax.Precision.HIGHEST
    t = (r * tr + lax.broadcasted_iota(jnp.int32, (tr, 1), 0)).astype(f32)
    tn = t / (seq_l - 1)
    ang = ((2.0 * math.pi / seq_l) * t) * bands_ref[...]
    lane = lax.broadcasted_iota(jnp.int32, (tr, LANE), 1)
    feat = jnp.where(lane < HY_BANDS, jnp.cos(ang),
                     jnp.where(lane < 2 * HY_BANDS, jnp.sin(ang),
                               jnp.where(lane == 2 * HY_BANDS, tn, 0.0)))
    sf = sf_ref[...]
    h = jnp.sin(sf[0:1] * (jnp.dot(feat, w1_ref[...], precision=hp, preferred_element_type=f32) + b1_ref[...]))
    h = jnp.sin(sf[1:2] * (jnp.dot(h, w2_ref[...], precision=hp, preferred_element_type=f32) + b2_ref[...]))
    h = jnp.dot(h, w3_ref[...], precision=hp, preferred_element_type=f32)
    o_ref[...] = h * jnp.exp(-tn * dec_ref[...])


def _hyfilter(bands, w1p, b1, sf, w2, b2, w3, dec, *, lp, seq_l):
    tr = _pick(lp, (384, 128))
    fh = w2.shape[0]
    wout = w3.shape[1]
    const = lambda r: (0, 0)
    return pl.pallas_call(
        functools.partial(_hyfilter_kernel, tr=tr, seq_l=seq_l),
        grid=(lp // tr,),
        in_specs=[pl.BlockSpec((1, LANE), const), pl.BlockSpec((LANE, fh), const),
                  pl.BlockSpec((1, fh), const), pl.BlockSpec((2, fh), const),
                  pl.BlockSpec((fh, fh), const), pl.BlockSpec((1, fh), const),
                  pl.BlockSpec((fh, wout), const), pl.BlockSpec((1, wout), const)],
        out_specs=pl.BlockSpec((tr, wout), lambda r: (r, 0)),
        out_shape=jax.ShapeDtypeStruct((lp, wout), f32),
        compiler_params=_cparams(("parallel",)),
        name="hyena_filter",
    )(bands, w1p, b1, sf, w2, b2, w3, dec)


HALO = 16


def _hypre_kernel(z_ref, zp_ref, zn_ref, cw_ref, cb_ref, u_ref, x1_ref, *, tr, nr, hw):
    r = pl.program_id(1)
    z = z_ref[...].astype(f32)
    prev_row = jnp.where(r > 0, zp_ref[HALO - 1:HALO, :].astype(f32), 0.0)
    next_row = jnp.where(r < nr - 1, zn_ref[0:1, :].astype(f32), 0.0)
    ridx = lax.broadcasted_iota(jnp.int32, (tr, 1), 0)
    z_prev = jnp.where(ridx == 0, prev_row, pltpu.roll(z, 1, axis=0))
    z_next = jnp.where(ridx == tr - 1, next_row, pltpu.roll(z, tr - 1, axis=0))
    cw = cw_ref[...]
    zc = z_prev * cw[0:1] + z * cw[1:2] + z_next * cw[2:3] + cb_ref[...]
    x0 = zc[:, :hw]
    x1 = zc[:, hw:2 * hw]
    vv = zc[:, 2 * hw:]
    pos = r * tr + ridx
    u_ref[...] = jnp.where(pos >= PADF, vv * x0, 0.0)
    x1_ref[...] = x1


def _hypre(z, cw, cb, *, nb, lp, hw):
    t, zw = z.shape
    tr = _pick(lp, (384, 128))
    nr = lp // tr
    per = tr // HALO
    nh = t // HALO
    row = lambda b, r: (b * nr + r, 0)
    const = lambda b, r: (0, 0)
    return pl.pallas_call(
        functools.partial(_hypre_kernel, tr=tr, nr=nr, hw=hw),
        grid=(nb, nr),
        in_specs=[pl.BlockSpec((tr, zw), row),
                  pl.BlockSpec((HALO, zw), lambda b, r: (jnp.maximum((b * nr + r) * per - 1, 0), 0)),
                  pl.BlockSpec((HALO, zw), lambda b, r: (jnp.minimum((b * nr + r + 1) * per, nh - 1), 0)),
                  pl.BlockSpec((3, zw), const), pl.BlockSpec((1, zw), const)],
        out_specs=[pl.BlockSpec((tr, hw), row), pl.BlockSpec((tr, hw), row)],
        out_shape=[jax.ShapeDtypeStruct((t, hw), f32), jax.ShapeDtypeStruct((t, hw), f32)],
        compiler_params=_cparams(("parallel", "parallel")),
        name="hyena_shortconv",
    )(z, z, z, cw, cb)


def _hyconv_kernel(u_ref, h_ref, y_ref, rl_ref, *, nj, nb):
    jj = lax.broadcasted_iota(jnp.int32, (LANE, LANE), 0)
    ii = lax.broadcasted_iota(jnp.int32, (LANE, LANE), 1)
    upper = ii >= jj
    y_ref[...] = jnp.zeros(y_ref.shape, f32)

    def rolled(r):
        row = jnp.broadcast_to(h_ref[0, r:r + 1, :], (LANE, LANE))
        return pltpu.roll(row, 0, axis=1, stride=1, stride_axis=0)

    rl_ref[...] = rolled(0)
    for d in range(-(nj - 1), nj):
        hi = rolled(d + nj)
        tile = jnp.where(upper, hi, rl_ref[...]).astype(bf16)
        rl_ref[...] = hi
        j0, j1 = max(0, -d), min(nj, nj - d)
        src = u_ref[0, j0 * nb:j1 * nb, :].astype(bf16)
        dst = slice((j0 + d) * nb, (j1 + d) * nb)
        y_ref[0, dst, :] += jnp.dot(src, tile, preferred_element_type=f32)


def _hyconv(ut, hc, *, nj, nb):
    c, rows, _ = ut.shape
    return pl.pallas_call(
        functools.partial(_hyconv_kernel, nj=nj, nb=nb),
        grid=(c,),
        in_specs=[pl.BlockSpec((1, rows, LANE), lambda i: (i, 0, 0)),
                  pl.BlockSpec((1, 2 * nj, LANE), lambda i: (i, 0, 0))],
        out_specs=pl.BlockSpec((1, rows, LANE), lambda i: (i, 0, 0)),
        out_shape=jax.ShapeDtypeStruct((c, rows, LANE), f32),
        scratch_shapes=[pltpu.VMEM((LANE, LANE), f32)],
        compiler_params=_cparams(("parallel",)),
        name="hyena_longconv",
    )(ut, hc)


def _outproj_kernel(h_ref, oa_ref, y_ref, u_ref, x1_ref, ds_ref, gh_ref, bd_ref, w_ref, o_ref):
    ohy = (y_ref[...] + u_ref[...] * ds_ref[...]) * x1_ref[...]
    ms = jnp.dot((ohy * ohy).astype(bf16), bd_ref[...], preferred_element_type=f32) * (1.0 / HEAD_DIM)
    ohn = (ohy * lax.rsqrt(ms + NORM_EPS) * gh_ref[...]).astype(bf16)
    mixed = jnp.concatenate([oa_ref[...], ohn], axis=1)
    o_ref[...] = h_ref[...] + jnp.dot(mixed, w_ref[...], preferred_element_type=f32)


def _outproj(h2d, oa, y, u, x1, dskip, gh, bd, w_out):
    t, d = h2d.shape
    aw = oa.shape[1]
    hw = y.shape[1]
    tr = _pick(t, (384, 128))
    row = lambda r: (r, 0)
    const = lambda r: (0, 0)
    return pl.pallas_call(
        _outproj_kernel,
        grid=(t // tr,),
        in_specs=[pl.BlockSpec((tr, d), row), pl.BlockSpec((tr, aw), row),
                  pl.BlockSpec((tr, hw), row), pl.BlockSpec((tr, hw), row), pl.BlockSpec((tr, hw), row),
                  pl.BlockSpec((1, hw), const), pl.BlockSpec((1, hw), const),
                  pl.BlockSpec((hw, hw), const), pl.BlockSpec((d, d), const)],
        out_specs=pl.BlockSpec((tr, d), row),
        out_shape=jax.ShapeDtypeStruct((t, d), f32),
        compiler_params=_cparams(("parallel",)),
        name="outproj",
    )(h2d, oa, y, u, x1, dskip, gh, bd, w_out)


NTOP = PEER_TOPK + 1
TOPPAD = 24


def _gelu_x2(x):
    return x * (1.0 + lax.erf(x * (1.0 / math.sqrt(2.0))))


SUBLANES = 8


def _sort_network(n):
    def merge(lo, hi, r):
        step = r * 2
        if step < hi - lo:
            yield from merge(lo, hi, step)
            yield from merge(lo + r, hi, step)
            for i in range(lo + r, hi - r, step):
                yield (i, i + r)
        else:
            yield (lo, lo + r)

    def sort(lo, hi):
        if hi - lo >= 1:
            mid = lo + (hi - lo) // 2
            yield from sort(lo, mid)
            yield from sort(mid + 1, hi)
            yield from merge(lo, hi, 1)

    return list(sort(0, n - 1))


def _top_values(s, top_ref, hp, cols):
    nt = s.shape[0] // SUBLANES
    v = [s[i * SUBLANES:(i + 1) * SUBLANES, :] for i in range(nt)]
    for i, j in _sort_network(nt):
        v[i], v[j] = jnp.maximum(v[i], v[j]), jnp.minimum(v[i], v[j])
    v.append(jnp.full((SUBLANES, LANE), -jnp.inf, f32))
    for t in range(NTOP):
        m = jnp.max(v[0], axis=0, keepdims=True)
        top_ref[hp, t:t + 1, cols] = m
        pop = v[0] == m
        for k in range(min(nt, NTOP - 1 - t)):
            v[k] = jnp.where(pop, v[k + 1], v[k])


CAND_SUB = 8
CAND_ROWS = TOPPAD + (CAND_SUB - 1) * CAND_SUB + 2 * CAND_SUB


def _peer_kernel(h_ref, g_ref, wq_ref, keys_ref, u_ref, vt_ref, o_ref,
                 hnt_ref, e1_ref, e2_ref, th_ref, top_ref, cand_ref, acc_ref,
                 act_ref, wa_ref, *, tb, eb, nheads, nkeys, ne):
    e = pl.program_id(1)
    neg = -jnp.inf

    @pl.when(e == 0)
    def _route():
        x = h_ref[...]
        ms = jnp.mean(x * x, axis=-1, keepdims=True)
        hn = x * lax.rsqrt(ms + NORM_EPS) * g_ref[...]
        hnt = hn.T.astype(bf16)
        hnt_ref[...] = hnt
        qt = jnp.dot(wq_ref[...], hnt, preferred_element_type=f32)
        dh = keys_ref.shape[2]
        for hp in range(2 * nheads):
            head, half = divmod(hp, 2)
            s = jnp.dot(keys_ref[hp], qt[hp * dh:(hp + 1) * dh, :].astype(bf16),
                        preferred_element_type=f32)
            for c in range(tb // LANE):
                _top_values(s[:, c * LANE:(c + 1) * LANE], top_ref, hp, slice(c * LANE, (c + 1) * LANE))
            top_ref[hp, NTOP:, :] = jnp.full((TOPPAD - NTOP, tb), neg, f32)
            smax = top_ref[hp, 0:1, :]
            ex = jnp.exp(s - smax)
            if half == 0:
                e1_ref[head] = ex
            else:
                e2_ref[head] = ex.astype(bf16)
            top_ref[hp] = jnp.exp(top_ref[hp] - smax)
        for head in range(nheads):
            ta, tb_ = 2 * head, 2 * head + 1
            cand_ref[0:TOPPAD, :] = top_ref[ta, 0:1, :] * top_ref[tb_]
            for k1 in range(1, CAND_SUB):
                r0 = TOPPAD + (k1 - 1) * CAND_SUB
                cand_ref[r0:r0 + CAND_SUB, :] = top_ref[ta, k1:k1 + 1, :] * top_ref[tb_, 0:CAND_SUB, :]
            r0 = TOPPAD + (CAND_SUB - 1) * CAND_SUB
            cand_ref[r0:, :] = top_ref[ta, CAND_SUB:, :] * top_ref[tb_, 0:1, :]
            tops = []
            for _ in range(NTOP):
                w = cand_ref[...]
                m = jnp.max(w, axis=0, keepdims=True)
                tops.append(m)
                cand_ref[...] = jnp.where(w == m, -1.0, w)
            zsum = tops[0]
            for k in range(1, PEER_TOPK):
                zsum = zsum + jnp.maximum(tops[k], 0.0)
            inv = 0.5 / zsum
            e1_ref[head] = e1_ref[head] * inv
            mid = 0.5 * (jnp.maximum(tops[PEER_TOPK - 1], 0.0) + jnp.maximum(tops[PEER_TOPK], 0.0))
            th_ref[head:head + 1, :] = mid * inv
        acc_ref[...] = jnp.zeros(acc_ref.shape, f32)

    ipb = eb // nkeys
    i0 = pl.multiple_of(e * ipb, ipb)
    nchunk = tb // LANE
    npiece = eb // MXU_K
    rows_per_piece = MXU_K // nkeys

    def routing_weights(ii, c):
        cols = slice(c * LANE, (c + 1) * LANE)
        w = None
        for head in range(nheads):
            e1row = e1_ref[head, pl.ds(i0, ipb), cols][ii:ii + 1, :].astype(bf16)
            p = e1row * e2_ref[head, :, cols]
            kept = jnp.where(p >= th_ref[head:head + 1, cols].astype(bf16), p, jnp.zeros_like(p))
            w = kept if w is None else w + kept
        return w

    act_ref[...] = jnp.dot(u_ref[...], hnt_ref[...], preferred_element_type=f32)
    for k in range(npiece):
        ks = slice(k * MXU_K, (k + 1) * MXU_K)
        for ii in range(k * rows_per_piece, (k + 1) * rows_per_piece):
            rows = slice(ii * nkeys, (ii + 1) * nkeys)
            for c in range(nchunk):
                cols = slice(c * LANE, (c + 1) * LANE)
                w = routing_weights(ii, c)
                wa_ref[rows, cols] = w * _gelu_x2(act_ref[rows, cols].astype(bf16))
        acc_ref[...] += jnp.dot(vt_ref[0, :, ks], wa_ref[ks, :], preferred_element_type=f32)

    @pl.when(e == ne - 1)
    def _finish():
        o_ref[...] = h_ref[...] + acc_ref[...].T


def _peer(h2d, g, wq_t, keys, u_tab, v_tab, *, nheads):
    t, d = h2d.shape
    nexp = u_tab.shape[0]
    nkeys = keys.shape[1]
    tb = _pick(t, (512, 384, 128))
    eb = 16 * nkeys
    qrows = wq_t.shape[0]
    ne = nexp // eb
    vt_tab = v_tab.reshape(ne, eb, d).transpose(0, 2, 1)
    tok = lambda i, e: (i, 0)
    const2 = lambda i, e: (0, 0)
    return pl.pallas_call(
        functools.partial(_peer_kernel, tb=tb, eb=eb, nheads=nheads, nkeys=nkeys, ne=ne),
        grid=(t // tb, ne),
        in_specs=[pl.BlockSpec((tb, d), tok),
                  pl.BlockSpec((1, d), const2),
                  pl.BlockSpec((qrows, d), const2),
                  pl.BlockSpec(keys.shape, lambda i, e: (0, 0, 0)),
                  pl.BlockSpec((eb, d), lambda i, e: (e, 0)),
                  pl.BlockSpec((1, d, eb), lambda i, e: (e, 0, 0))],
        out_specs=pl.BlockSpec((tb, d), tok),
        out_shape=jax.ShapeDtypeStruct((t, d), f32),
        scratch_shapes=[pltpu.VMEM((d, tb), bf16),
                        pltpu.VMEM((nheads, nkeys, tb), f32),
                        pltpu.VMEM((nheads, nkeys, tb), bf16),
                        pltpu.VMEM((nheads, tb), f32),
                        pltpu.VMEM((2 * nheads, TOPPAD, tb), f32),
                        pltpu.VMEM((CAND_ROWS, tb), f32),
                        pltpu.VMEM((d, tb), f32),
                        pltpu.VMEM((eb, tb), f32),
                        pltpu.VMEM((eb, tb), bf16)],
        compiler_params=_cparams(("parallel", "arbitrary")),
        name="peer",
    )(h2d, g, wq_t, keys, u_tab, vt_tab)


def _final_kernel(h_ref, g_ref, o_ref):
    x = h_ref[0]
    ms = jnp.mean(x * x, axis=-1, keepdims=True)
    o_ref[0] = x * lax.rsqrt(ms + NORM_EPS) * g_ref[...]


def _final(h3d, g, *, b0, nbatch, s):
    _, lp, d = h3d.shape
    tr = LANE
    return pl.pallas_call(
        _final_kernel,
        grid=(nbatch, s // tr),
        in_specs=[pl.BlockSpec((1, tr, d), lambda b, r: (b0 + b, r + 1, 0)),
                  pl.BlockSpec((1, d), lambda b, r: (0, 0))],
        out_specs=pl.BlockSpec((1, tr, d), lambda b, r: (b, r, 0)),
        out_shape=jax.ShapeDtypeStruct((nbatch, s, d), f32),
        compiler_params=_cparams(("parallel", "parallel")),
        name="final_norm",
    )(h3d, g)


def _block_diag(n):
    a = jnp.arange(n) // HEAD_DIM
    return (a[:, None] == a[None, :]).astype(bf16)


def _rope_tables(s, lp, gq, gk, n_heads, n_kv):
    rows = s // GRID_W
    row = jnp.repeat(jnp.arange(rows, dtype=f32), GRID_W)
    col = jnp.tile(jnp.arange(GRID_W, dtype=f32), rows)
    per_axis = HEAD_DIM // 4
    inv = ROPE_THETA ** (-jnp.arange(per_axis, dtype=f32) / per_axis)
    ang = jnp.concatenate([row[:, None] * inv, col[:, None] * inv], axis=-1)
    ang = jnp.concatenate([jnp.zeros((lp - s, HEAD_DIM // 2), f32), ang], axis=0)
    cos, sin = jnp.cos(ang), jnp.sin(ang)
    cfull = jnp.repeat(cos, 2, axis=1)
    sfull = jnp.stack([-sin, sin], axis=-1).reshape(lp, HEAD_DIM)
    swap = jnp.arange(HEAD_DIM) ^ 1
    scale = HEAD_DIM ** -0.5 * math.log2(math.e)
    cq, sq = cfull * gq * scale, sfull * gq[swap] * scale
    ck, sk = cfull * gk, sfull * gk[swap]
    ctab = jnp.concatenate([jnp.tile(cq, (1, n_heads)), jnp.tile(ck, (1, n_kv))], axis=1)
    stab = jnp.concatenate([jnp.tile(sq, (1, n_heads)), jnp.tile(sk, (1, n_kv))], axis=1)
    return ctab, stab


def kernel(x_prompt, x_sample, meta_tokens, norm1_g, w_in, q_norm_g, k_norm_g, hy_conv_w, hy_conv_b, hy_ffn_w1, hy_ffn_b1, hy_sin_freq, hy_ffn_w2, hy_ffn_b2, hy_ffn_w3, hy_decay, hy_dskip, attn_out_g, hy_out_g, w_out, norm2_g, peer_wq, peer_keys, peer_u, peer_v, final_g):
    depth, d, _ = w_in.shape
    b_p, s, _ = x_prompt.shape
    b_s = x_sample.shape[0]
    assert x_sample.shape[1] == s and s % LANE == 0 and s % GRID_W == 0
    nb = b_p + b_s
    seq_l = s + N_META
    lp = s + LANE
    nj = lp // LANE
    t = nb * lp
    n_heads = attn_out_g.shape[1]
    aw = n_heads * HEAD_DIM
    hw = d - aw
    kw = (w_in.shape[2] - aw - 3 * hw) // 2
    n_kv = kw // HEAD_DIM
    assert kw == LANE, "k/v lane-pair layout assumes two kv heads"
    peer_heads = peer_keys.shape[2]
    nkeys = peer_keys.shape[3]
    assert nkeys == LANE

    x = jnp.concatenate([x_prompt, x_sample], axis=0)
    front = jnp.concatenate([jnp.zeros((PADF, d), f32), meta_tokens.astype(f32)], axis=0)
    h = jnp.concatenate([jnp.broadcast_to(front[None], (nb, LANE, d)), x], axis=1).reshape(t, d)

    bd_qk = _block_diag(aw + kw)
    bd_hy = _block_diag(hw)
    swap_q = jnp.arange(aw) ^ 1
    swap_k = jnp.arange(kw) ^ 1
    bands = jnp.linspace(1e-4, HY_BANDS - 1, HY_BANDS, dtype=f32)
    bands_p = jnp.zeros((1, LANE), f32).at[0, :HY_BANDS].set(bands).at[0, HY_BANDS:2 * HY_BANDS].set(bands)

    for l in range(depth):
        w = w_in[l]
        wq, wk = w[:, :aw], w[:, aw:aw + kw]
        wv, wz = w[:, aw + kw:aw + 2 * kw], w[:, aw + 2 * kw:]
        w_ext = jnp.concatenate([wq, wk, wq[:, swap_q], wk[:, swap_k], wv, wz], axis=1).astype(bf16)
        ctab, stab = _rope_tables(s, lp, q_norm_g[l], k_norm_g[l], n_heads, n_kv)
        q, kx, vx, z = _inproj(h, norm1_g[l][None, :], w_ext, bd_qk, ctab, stab,
                               nb=nb, lp=lp, qw=aw, kw=kw, zw=3 * hw)
        o_att = _attention(q, kx, vx, attn_out_g[l].reshape(1, aw), nb=nb, lp=lp, n_kv=n_kv)

        w1 = hy_ffn_w1[l]
        fh = w1.shape[1]
        w1p = jnp.zeros((LANE, fh), f32).at[:2 * HY_BANDS].set(w1[1:]).at[2 * HY_BANDS].set(w1[0])
        filt = _hyfilter(bands_p, w1p, hy_ffn_b1[l][None], hy_sin_freq[l], hy_ffn_w2[l], hy_ffn_b2[l][None],
                         hy_ffn_w3[l], hy_decay[l].reshape(1, 2 * hw), lp=lp, seq_l=seq_l)
        hf, hb = filt[:seq_l, :hw], filt[:seq_l, hw:]
        lags = jnp.concatenate([jnp.zeros((lp - seq_l + 1, hw), f32), hb[:0:-1], hf,
                                jnp.zeros((lp - seq_l, hw), f32)], axis=0)
        hc = lags.T.reshape(hw, 2 * nj, LANE)

        u, x1 = _hypre(z, hy_conv_w[l], hy_conv_b[l][None], nb=nb, lp=lp, hw=hw)
        ut = u.reshape(nb, nj, LANE, hw).transpose(3, 1, 0, 2).reshape(hw, nj * nb, LANE)
        yt = _hyconv(ut, hc, nj=nj, nb=nb)
        y = yt.reshape(hw, nj, nb, LANE).transpose(2, 1, 3, 0).reshape(t, hw)

        h = _outproj(h, o_att, y, u, x1, hy_dskip[l][None], hy_out_g[l].reshape(1, hw), bd_hy,
                     w_out[l].astype(bf16))

        dh = peer_keys.shape[4]
        keys = peer_keys[l].transpose(1, 0, 2, 3).reshape(2 * peer_heads, nkeys, dh).astype(bf16)
        h = _peer(h, norm2_g[l][None], peer_wq[l].T.astype(bf16), keys,
                  peer_u[l].astype(bf16), peer_v[l].astype(bf16), nheads=peer_heads)

    h3 = h.reshape(nb, lp, d)
    y_prompt = _final(h3, final_g[None], b0=0, nbatch=b_p, s=s)
    y_sample = _final(h3, final_g[None], b0=b_p, nbatch=b_s, s=s)
    return (y_prompt, y_sample)
```

```python
import functools
import math

import jax
import jax.numpy as jnp
from jax import lax
from jax.experimental import pallas as pl
from jax.experimental.pallas import tpu as pltpu

N_META = 16
GRID_W = 64
HEAD_DIM = 64
ROPE_THETA = 10000.0
HY_BANDS = 16
PEER_TOPK = 16
NORM_EPS = 1e-6
LANE = 128
MXU_K = 256
PADF = LANE - N_META
VMEM_LIMIT = 56 * 1024 * 1024

f32 = jnp.float32
bf16 = jnp.bfloat16


def _cparams(sem):
    return pltpu.CompilerParams(dimension_semantics=sem, vmem_limit_bytes=VMEM_LIMIT)


def _pick(n, cands):
    for c in cands:
        if n % c == 0:
            return c
    raise ValueError(f"no block size for {n}")


def _inproj_kernel(x_ref, g_ref, w_ref, bd_ref, c_ref, s_ref,
                   q_ref, k_ref, v_ref, z_ref, *, qw, kw, tr):
    r = pl.program_id(1)
    x = x_ref[...]
    ms = jnp.mean(x * x, axis=-1, keepdims=True)
    hn = (x * lax.rsqrt(ms + NORM_EPS) * g_ref[...]).astype(bf16)
    proj = jnp.dot(hn, w_ref[...], preferred_element_type=f32)
    qkw = qw + kw
    qk = proj[:, :qkw]
    qk_swapped = proj[:, qkw:2 * qkw]
    msq = jnp.dot((qk * qk).astype(bf16), bd_ref[...], preferred_element_type=f32) * (1.0 / HEAD_DIM)
    qkr = lax.rsqrt(msq + NORM_EPS) * (qk * c_ref[...] + qk_swapped * s_ref[...])
    q_ref[...] = qkr[:, :qw].astype(bf16)

    lane = lax.broadcasted_iota(jnp.int32, (tr, LANE), 1)
    lo = lane < HEAD_DIM

    def ext(a):
        ar = pltpu.roll(a, HEAD_DIM, axis=1)
        zero = jnp.zeros_like(a)
        return jnp.concatenate([jnp.where(lo, a, zero), jnp.where(lo, zero, ar),
                                jnp.where(lo, ar, zero), jnp.where(lo, zero, a)], axis=1)

    k_ref[...] = ext(qkr[:, qw:qkw]).astype(bf16)
    v_ref[...] = ext(proj[:, 2 * qkw:2 * qkw + kw]).astype(bf16)
    z = proj[:, 2 * qkw + kw:]
    pos = r * tr + lax.broadcasted_iota(jnp.int32, (tr, 1), 0)
    z_ref[...] = jnp.where(pos >= PADF, z, 0.0).astype(bf16)


def _inproj(h2d, g, w_ext, bd, ctab, stab, *, nb, lp, qw, kw, zw):
    d = h2d.shape[1]
    tr = _pick(lp, (384, 128))
    nr = lp // tr
    t = nb * lp
    wcols = w_ext.shape[1]
    qkw = qw + kw
    row = lambda b, r: (b * nr + r, 0)
    const = lambda b, r: (0, 0)
    return pl.pallas_call(
        functools.partial(_inproj_kernel, qw=qw, kw=kw, tr=tr),
        grid=(nb, nr),
        in_specs=[pl.BlockSpec((tr, d), row),
                  pl.BlockSpec((1, d), const),
                  pl.BlockSpec((d, wcols), const),
                  pl.BlockSpec((qkw, qkw), const),
                  pl.BlockSpec((tr, qkw), lambda b, r: (r, 0)),
                  pl.BlockSpec((tr, qkw), lambda b, r: (r, 0))],
        out_specs=[pl.BlockSpec((tr, qw), row),
                   pl.BlockSpec((tr, 4 * kw), row),
                   pl.BlockSpec((tr, 4 * kw), row),
                   pl.BlockSpec((tr, zw), row)],
        out_shape=[jax.ShapeDtypeStruct((t, qw), bf16),
                   jax.ShapeDtypeStruct((t, 4 * kw), bf16),
                   jax.ShapeDtypeStruct((t, 4 * kw), bf16),
                   jax.ShapeDtypeStruct((t, zw), bf16)],
        compiler_params=_cparams(("parallel", "parallel")),
        name="inproj",
    )(h2d, g, w_ext, bd, ctab, stab)


SM_ROWS = 16


def _attn_kernel(q_ref, k_ref, v_ref, g_ref, o_ref, s_a, s_b, p_a, p_b, rl_ref, *, tq):
    lane = lax.broadcasted_iota(jnp.int32, (tq, LANE), 1)
    lo = lane < HEAD_DIM
    real = lax.broadcasted_iota(jnp.int32, (SM_ROWS, LANE), 1) >= PADF
    nt = (((1,), (1,)), ((), ()))
    heads = [(pr, hh) for pr in range(q_ref.shape[1] // LANE) for hh in range(2)]
    s_bufs, p_bufs = (s_a, s_b), (p_a, p_b)

    def scores(n):
        pr, hh = heads[n]
        s_bufs[n % 2][...] = lax.dot_general(q_ref[:, pr * LANE:(pr + 1) * LANE],
                                             k_ref[:, hh * LANE:(hh + 1) * LANE], nt,
                                             preferred_element_type=f32)

    def softmax_rows(s_ref, p_ref, r):
        rows = slice(r * SM_ROWS, (r + 1) * SM_ROWS)
        s0 = jnp.where(real, s_ref[rows, :LANE], -jnp.inf)
        s1 = s_ref[rows, LANE:]
        m = jnp.maximum(jnp.max(s0, axis=-1, keepdims=True), jnp.max(s1, axis=-1, keepdims=True))
        p0 = jnp.exp2(s0 - m)
        p1 = jnp.exp2(s1 - m)
        l = jnp.sum(p0, axis=-1, keepdims=True) + jnp.sum(p1, axis=-1, keepdims=True)
        p_ref[rows, :LANE] = p0.astype(bf16)
        p_ref[rows, LANE:] = p1.astype(bf16)
        rl_ref[rows, :] = jnp.broadcast_to(1.0 / l, (SM_ROWS, LANE))

    scores(0)
    for n, (pr, hh) in enumerate(heads):
        cols = slice(pr * LANE, (pr + 1) * LANE)
        if n + 1 < len(heads):
            scores(n + 1)
        for r in range(tq // SM_ROWS):
            softmax_rows(s_bufs[n % 2], p_bufs[n % 2], r)
        oh = jnp.dot(p_bufs[n % 2][...], v_ref[:, hh * LANE:(hh + 1) * LANE],
                     preferred_element_type=f32) * rl_ref[...]
        if hh == 0:
            o = oh
            continue
        o = o + oh
        o2 = o * o
        s_all = jnp.sum(o2, axis=-1, keepdims=True)
        s_lo = jnp.sum(jnp.where(lo, o2, 0.0), axis=-1, keepdims=True)
        ms = jnp.where(lo, s_lo, s_all - s_lo) * (1.0 / HEAD_DIM)
        o_ref[:, cols] = (o * lax.rsqrt(ms + NORM_EPS) * g_ref[:, cols]).astype(bf16)


def _attention(q, kx, vx, g_out, *, nb, lp, n_kv):
    t, qw = q.shape
    gw = qw // n_kv
    assert gw % LANE == 0
    tq = _pick(lp, (384, 128))
    nq = lp // tq
    return pl.pallas_call(
        functools.partial(_attn_kernel, tq=tq),
        grid=(nb, n_kv, nq),
        in_specs=[pl.BlockSpec((tq, gw), lambda b, h, i: (b * nq + i, h)),
                  pl.BlockSpec((lp, 2 * LANE), lambda b, h, i: (b, h)),
                  pl.BlockSpec((lp, 2 * LANE), lambda b, h, i: (b, h)),
                  pl.BlockSpec((1, gw), lambda b, h, i: (0, h))],
        out_specs=pl.BlockSpec((tq, gw), lambda b, h, i: (b * nq + i, h)),
        out_shape=jax.ShapeDtypeStruct((t, qw), bf16),
        scratch_shapes=[pltpu.VMEM((tq, lp), f32), pltpu.VMEM((tq, lp), f32),
                        pltpu.VMEM((tq, lp), bf16), pltpu.VMEM((tq, lp), bf16),
                        pltpu.VMEM((tq, LANE), f32)],
        compiler_params=_cparams(("parallel", "parallel", "parallel")),
        name="attention",
    )(q, kx, vx, g_out)


def _hyfilter_kernel(bands_ref, w1_ref, b1_ref, sf_ref, w2_ref, b2_ref, w3_ref, dec_ref, o_ref,
                     *, tr, seq_l):
    r = pl.program_id(0)
    hp = lax.Precision.HIGHEST
    t = (r * tr + lax.broadcasted_iota(jnp.int32, (tr, 1), 0)).astype(f32)
    tn = t / (seq_l - 1)
    ang = ((2.0 * math.pi / seq_l) * t) * bands_ref[...]
    lane = lax.broadcasted_iota(jnp.int32, (tr, LANE), 1)
    feat = jnp.where(lane < HY_BANDS, jnp.cos(ang),
                     jnp.where(lane < 2 * HY_BANDS, jnp.sin(ang),
                               jnp.where(lane == 2 * HY_BANDS, tn, 0.0)))
    sf = sf_ref[...]
    h = jnp.sin(sf[0:1] * (jnp.dot(feat, w1_ref[...], precision=hp, preferred_element_type=f32) + b1_ref[...]))
    h = jnp.sin(sf[1:2] * (jnp.dot(h, w2_ref[...], precision=hp, preferred_element_type=f32) + b2_ref[...]))
    h = jnp.dot(h, w3_ref[...], precision=hp, preferred_element_type=f32)
    o_ref[...] = h * jnp.exp(-tn * dec_ref[...])


def _hyfilter(bands, w1p, b1, sf, w2, b2, w3, dec, *, lp, seq_l):
    tr = _pick(lp, (384, 128))
    fh = w2.shape[0]
    wout = w3.shape[1]
    const = lambda r: (0, 0)
    return pl.pallas_call(
        functools.partial(_hyfilter_kernel, tr=tr, seq_l=seq_l),
        grid=(lp // tr,),
        in_specs=[pl.BlockSpec((1, LANE), const), pl.BlockSpec((LANE, fh), const),
                  pl.BlockSpec((1, fh), const), pl.BlockSpec((2, fh), const),
                  pl.BlockSpec((fh, fh), const), pl.BlockSpec((1, fh), const),
                  pl.BlockSpec((fh, wout), const), pl.BlockSpec((1, wout), const)],
        out_specs=pl.BlockSpec((tr, wout), lambda r: (r, 0)),
        out_shape=jax.ShapeDtypeStruct((lp, wout), f32),
        compiler_params=_cparams(("parallel",)),
        name="hyena_filter",
    )(bands, w1p, b1, sf, w2, b2, w3, dec)


HALO = 16


def _hypre_kernel(z_ref, zp_ref, zn_ref, cw_ref, cb_ref, u_ref, x1_ref, *, tr, nr, hw):
    r = pl.program_id(1)
    z = z_ref[...].astype(f32)
    prev_row = jnp.where(r > 0, zp_ref[HALO - 1:HALO, :].astype(f32), 0.0)
    next_row = jnp.where(r < nr - 1, zn_ref[0:1, :].astype(f32), 0.0)
    ridx = lax.broadcasted_iota(jnp.int32, (tr, 1), 0)
    z_prev = jnp.where(ridx == 0, prev_row, pltpu.roll(z, 1, axis=0))
    z_next = jnp.where(ridx == tr - 1, next_row, pltpu.roll(z, tr - 1, axis=0))
    cw = cw_ref[...]
    zc = z_prev * cw[0:1] + z * cw[1:2] + z_next * cw[2:3] + cb_ref[...]
    x0 = zc[:, :hw]
    x1 = zc[:, hw:2 * hw]
    vv = zc[:, 2 * hw:]
    pos = r * tr + ridx
    u_ref[...] = jnp.where(pos >= PADF, vv * x0, 0.0)
    x1_ref[...] = x1


def _hypre(z, cw, cb, *, nb, lp, hw):
    t, zw = z.shape
    tr = _pick(lp, (384, 128))
    nr = lp // tr
    per = tr // HALO
    nh = t // HALO
    row = lambda b, r: (b * nr + r, 0)
    const = lambda b, r: (0, 0)
    return pl.pallas_call(
        functools.partial(_hypre_kernel, tr=tr, nr=nr, hw=hw),
        grid=(nb, nr),
        in_specs=[pl.BlockSpec((tr, zw), row),
                  pl.BlockSpec((HALO, zw), lambda b, r: (jnp.maximum((b * nr + r) * per - 1, 0), 0)),
                  pl.BlockSpec((HALO, zw), lambda b, r: (jnp.minimum((b * nr + r + 1) * per, nh - 1), 0)),
                  pl.BlockSpec((3, zw), const), pl.BlockSpec((1, zw), const)],
        out_specs=[pl.BlockSpec((tr, hw), row), pl.BlockSpec((tr, hw), row)],
        out_shape=[jax.ShapeDtypeStruct((t, hw), f32), jax.ShapeDtypeStruct((t, hw), f32)],
        compiler_params=_cparams(("parallel", "parallel")),
        name="hyena_shortconv",
    )(z, z, z, cw, cb)


def _hyconv_kernel(u_ref, h_ref, y_ref, rl_ref, *, nj, nb):
    jj = lax.broadcasted_iota(jnp.int32, (LANE, LANE), 0)
    ii = lax.broadcasted_iota(jnp.int32, (LANE, LANE), 1)
    upper = ii >= jj
    y_ref[...] = jnp.zeros(y_ref.shape, f32)

    def rolled(r):
        row = jnp.broadcast_to(h_ref[0, r:r + 1, :], (LANE, LANE))
        return pltpu.roll(row, 0, axis=1, stride=1, stride_axis=0)

    rl_ref[...] = rolled(0)
    for d in range(-(nj - 1), nj):
        hi = rolled(d + nj)
        tile = jnp.where(upper, hi, rl_ref[...]).astype(bf16)
        rl_ref[...] = hi
        j0, j1 = max(0, -d), min(nj, nj - d)
        src = u_ref[0, j0 * nb:j1 * nb, :].astype(bf16)
        dst = slice((j0 + d) * nb, (j1 + d) * nb)
        y_ref[0, dst, :] += jnp.dot(src, tile, preferred_element_type=f32)


def _hyconv(ut, hc, *, nj, nb):
    c, rows, _ = ut.shape
    return pl.pallas_call(
        functools.partial(_hyconv_kernel, nj=nj, nb=nb),
        grid=(c,),
        in_specs=[pl.BlockSpec((1, rows, LANE), lambda i: (i, 0, 0)),
                  pl.BlockSpec((1, 2 * nj, LANE), lambda i: (i, 0, 0))],
        out_specs=pl.BlockSpec((1, rows, LANE), lambda i: (i, 0, 0)),
        out_shape=jax.ShapeDtypeStruct((c, rows, LANE), f32),
        scratch_shapes=[pltpu.VMEM((LANE, LANE), f32)],
        compiler_params=_cparams(("parallel",)),
        name="hyena_longconv",
    )(ut, hc)


def _outproj_kernel(h_ref, oa_ref, y_ref, u_ref, x1_ref, ds_ref, gh_ref, bd_ref, w_ref, o_ref):
    ohy = (y_ref[...] + u_ref[...] * ds_ref[...]) * x1_ref[...]
    ms = jnp.dot((ohy * ohy).astype(bf16), bd_ref[...], preferred_element_type=f32) * (1.0 / HEAD_DIM)
    ohn = (ohy * lax.rsqrt(ms + NORM_EPS) * gh_ref[...]).astype(bf16)
    mixed = jnp.concatenate([oa_ref[...], ohn], axis=1)
    o_ref[...] = h_ref[...] + jnp.dot(mixed, w_ref[...], preferred_element_type=f32)


def _outproj(h2d, oa, y, u, x1, dskip, gh, bd, w_out):
    t, d = h2d.shape
    aw = oa.shape[1]
    hw = y.shape[1]
    tr = _pick(t, (384, 128))
    row = lambda r: (r, 0)
    const = lambda r: (0, 0)
    return pl.pallas_call(
        _outproj_kernel,
        grid=(t // tr,),
        in_specs=[pl.BlockSpec((tr, d), row), pl.BlockSpec((tr, aw), row),
                  pl.BlockSpec((tr, hw), row), pl.BlockSpec((tr, hw), row), pl.BlockSpec((tr, hw), row),
                  pl.BlockSpec((1, hw), const), pl.BlockSpec((1, hw), const),
                  pl.BlockSpec((hw, hw), const), pl.BlockSpec((d, d), const)],
        out_specs=pl.BlockSpec((tr, d), row),
        out_shape=jax.ShapeDtypeStruct((t, d), f32),
        compiler_params=_cparams(("parallel",)),
        name="outproj",
    )(h2d, oa, y, u, x1, dskip, gh, bd, w_out)


NTOP = PEER_TOPK + 1
TOPPAD = 24


def _gelu_x2(x):
    return x * (1.0 + lax.erf(x * (1.0 / math.sqrt(2.0))))


SUBLANES = 8


def _sort_network(n):
    def merge(lo, hi, r):
        step = r * 2
        if step < hi - lo:
            yield from merge(lo, hi, step)
            yield from merge(lo + r, hi, step)
            for i in range(lo + r, hi - r, step):
                yield (i, i + r)
        else:
            yield (lo, lo + r)

    def sort(lo, hi):
        if hi - lo >= 1:
            mid = lo + (hi - lo) // 2
            yield from sort(lo, mid)
            yield from sort(mid + 1, hi)
            yield from merge(lo, hi, 1)

    return list(sort(0, n - 1))


def _top_values(s, top_ref, hp, cols):
    nt = s.shape[0] // SUBLANES
    v = [s[i * SUBLANES:(i + 1) * SUBLANES, :] for i in range(nt)]
    for i, j in _sort_network(nt):
        v[i], v[j] = jnp.maximum(v[i], v[j]), jnp.minimum(v[i], v[j])
    v.append(jnp.full((SUBLANES, LANE), -jnp.inf, f32))
    for t in range(NTOP):
        m = jnp.max(v[0], axis=0, keepdims=True)
        top_ref[hp, t:t + 1, cols] = m
        pop = v[0] == m
        for k in range(min(nt, NTOP - 1 - t)):
            v[k] = jnp.where(pop, v[k + 1], v[k])


CAND_SUB = 8
CAND_ROWS = TOPPAD + (CAND_SUB - 1) * CAND_SUB + 2 * CAND_SUB


def _peer_kernel(h_ref, g_ref, wq_ref, keys_ref, u_ref, vt_ref, o_ref,
                 hnt_ref, e1_ref, e2_ref, th_ref, top_ref, cand_ref, acc_ref,
                 act_ref, wa_ref, *, tb, eb, nheads, nkeys, ne):
    e = pl.program_id(1)
    neg = -jnp.inf

    @pl.when(e == 0)
    def _route():
        x = h_ref[...]
        ms = jnp.mean(x * x, axis=-1, keepdims=True)
        hn = x * lax.rsqrt(ms + NORM_EPS) * g_ref[...]
        hnt = hn.T.astype(bf16)
        hnt_ref[...] = hnt
        qt = jnp.dot(wq_ref[...], hnt, preferred_element_type=f32)
        dh = keys_ref.shape[2]
        for hp in range(2 * nheads):
            head, half = divmod(hp, 2)
            s = jnp.dot(keys_ref[hp], qt[hp * dh:(hp + 1) * dh, :].astype(bf16),
                        preferred_element_type=f32)
            for c in range(tb // LANE):
                _top_values(s[:, c * LANE:(c + 1) * LANE], top_ref, hp, slice(c * LANE, (c + 1) * LANE))
            top_ref[hp, NTOP:, :] = jnp.full((TOPPAD - NTOP, tb), neg, f32)
            smax = top_ref[hp, 0:1, :]
            ex = jnp.exp(s - smax)
            if half == 0:
                e1_ref[head] = ex
            else:
                e2_ref[head] = ex.astype(bf16)
            top_ref[hp] = jnp.exp(top_ref[hp] - smax)
        for head in range(nheads):
            ta, tb_ = 2 * head, 2 * head + 1
            cand_ref[0:TOPPAD, :] = top_ref[ta, 0:1, :] * top_ref[tb_]
            for k1 in range(1, CAND_SUB):
                r0 = TOPPAD + (k1 - 1) * CAND_SUB
                cand_ref[r0:r0 + CAND_SUB, :] = top_ref[ta, k1:k1 + 1, :] * top_ref[tb_, 0:CAND_SUB, :]
            r0 = TOPPAD + (CAND_SUB - 1) * CAND_SUB
            cand_ref[r0:, :] = top_ref[ta, CAND_SUB:, :] * top_ref[tb_, 0:1, :]
            tops = []
            for _ in range(NTOP):
                w = cand_ref[...]
                m = jnp.max(w, axis=0, keepdims=True)
                tops.append(m)
                cand_ref[...] = jnp.where(w == m, -1.0, w)
            zsum = tops[0]
            for k in range(1, PEER_TOPK):
                zsum = zsum + jnp.maximum(tops[k], 0.0)
            inv = 0.5 / zsum
            e1_ref[head] = e1_ref[head] * inv
            mid = 0.5 * (jnp.maximum(tops[PEER_TOPK - 1], 0.0) + jnp.maximum(tops[PEER_TOPK], 0.0))
            th_ref[head:head + 1, :] = mid * inv
        acc_ref[...] = jnp.zeros(acc_ref.shape, f32)

    ipb = eb // nkeys
    i0 = pl.multiple_of(e * ipb, ipb)
    nchunk = tb // LANE
    npiece = eb // MXU_K
    rows_per_piece = MXU_K // nkeys

    def routing_weights(ii, c):
        cols = slice(c * LANE, (c + 1) * LANE)
        w = None
        for head in range(nheads):
            e1row = e1_ref[head, pl.ds(i0, ipb), cols][ii:ii + 1, :].astype(bf16)
            p = e1row * e2_ref[head, :, cols]
            kept = jnp.where(p >= th_ref[head:head + 1, cols].astype(bf16), p, jnp.zeros_like(p))
            w = kept if w is None else w + kept
        return w

    act_ref[...] = jnp.dot(u_ref[...], hnt_ref[...], preferred_element_type=f32)
    for k in range(npiece):
        ks = slice(k * MXU_K, (k + 1) * MXU_K)
        for ii in range(k * rows_per_piece, (k + 1) * rows_per_piece):
            rows = slice(ii * nkeys, (ii + 1) * nkeys)
            for c in range(nchunk):
                cols = slice(c * LANE, (c + 1) * LANE)
                w = routing_weights(ii, c)
                wa_ref[rows, cols] = w * _gelu_x2(act_ref[rows, cols].astype(bf16))
        acc_ref[...] += jnp.dot(vt_ref[0, :, ks], wa_ref[ks, :], preferred_element_type=f32)

    @pl.when(e == ne - 1)
    def _finish():
        o_ref[...] = h_ref[...] + acc_ref[...].T


def _peer(h2d, g, wq_t, keys, u_tab, v_tab, *, nheads):
    t, d = h2d.shape
    nexp = u_tab.shape[0]
    nkeys = keys.shape[1]
    tb = _pick(t, (512, 384, 128))
    eb = 8 * nkeys
    qrows = wq_t.shape[0]
    ne = nexp // eb
    vt_tab = v_tab.reshape(ne, eb, d).transpose(0, 2, 1)
    tok = lambda i, e: (i, 0)
    const2 = lambda i, e: (0, 0)
    return pl.pallas_call(
        functools.partial(_peer_kernel, tb=tb, eb=eb, nheads=nheads, nkeys=nkeys, ne=ne),
        grid=(t // tb, ne),
        in_specs=[pl.BlockSpec((tb, d), tok),
                  pl.BlockSpec((1, d), const2),
                  pl.BlockSpec((qrows, d), const2),
                  pl.BlockSpec(keys.shape, lambda i, e: (0, 0, 0)),
                  pl.BlockSpec((eb, d), lambda i, e: (e, 0)),
                  pl.BlockSpec((1, d, eb), lambda i, e: (e, 0, 0))],
        out_specs=pl.BlockSpec((tb, d), tok),
        out_shape=jax.ShapeDtypeStruct((t, d), f32),
        scratch_shapes=[pltpu.VMEM((d, tb), bf16),
                        pltpu.VMEM((nheads, nkeys, tb), f32),
                        pltpu.VMEM((nheads, nkeys, tb), bf16),
                        pltpu.VMEM((nheads, tb), f32),
                        pltpu.VMEM((2 * nheads, TOPPAD, tb), f32),
                        pltpu.VMEM((CAND_ROWS, tb), f32),
                        pltpu.VMEM((d, tb), f32),
                        pltpu.VMEM((eb, tb), f32),
                        pltpu.VMEM((eb, tb), bf16)],
        compiler_params=_cparams(("parallel", "arbitrary")),
        name="peer",
    )(h2d, g, wq_t, keys, u_tab, vt_tab)


def _final_kernel(h_ref, g_ref, o_ref):
    x = h_ref[0]
    ms = jnp.mean(x * x, axis=-1, keepdims=True)
    o_ref[0] = x * lax.rsqrt(ms + NORM_EPS) * g_ref[...]


def _final(h3d, g, *, b0, nbatch, s):
    _, lp, d = h3d.shape
    tr = LANE
    return pl.pallas_call(
        _final_kernel,
        grid=(nbatch, s // tr),
        in_specs=[pl.BlockSpec((1, tr, d), lambda b, r: (b0 + b, r + 1, 0)),
                  pl.BlockSpec((1, d), lambda b, r: (0, 0))],
        out_specs=pl.BlockSpec((1, tr, d), lambda b, r: (b, r, 0)),
        out_shape=jax.ShapeDtypeStruct((nbatch, s, d), f32),
        compiler_params=_cparams(("parallel", "parallel")),
        name="final_norm",
    )(h3d, g)


def _block_diag(n):
    a = jnp.arange(n) // HEAD_DIM
    return (a[:, None] == a[None, :]).astype(bf16)


def _rope_tables(s, lp, gq, gk, n_heads, n_kv):
    rows = s // GRID_W
    row = jnp.repeat(jnp.arange(rows, dtype=f32), GRID_W)
    col = jnp.tile(jnp.arange(GRID_W, dtype=f32), rows)
    per_axis = HEAD_DIM // 4
    inv = ROPE_THETA ** (-jnp.arange(per_axis, dtype=f32) / per_axis)
    ang = jnp.concatenate([row[:, None] * inv, col[:, None] * inv], axis=-1)
    ang = jnp.concatenate([jnp.zeros((lp - s, HEAD_DIM // 2), f32), ang], axis=0)
    cos, sin = jnp.cos(ang), jnp.sin(ang)
    cfull = jnp.repeat(cos, 2, axis=1)
    sfull = jnp.stack([-sin, sin], axis=-1).reshape(lp, HEAD_DIM)
    swap = jnp.arange(HEAD_DIM) ^ 1
    scale = HEAD_DIM ** -0.5 * math.log2(math.e)
    cq, sq = cfull * gq * scale, sfull * gq[swap] * scale
    ck, sk = cfull * gk, sfull * gk[swap]
    ctab = jnp.concatenate([jnp.tile(cq, (1, n_heads)), jnp.tile(ck, (1, n_kv))], axis=1)
    stab = jnp.concatenate([jnp.tile(sq, (1, n_heads)), jnp.tile(sk, (1, n_kv))], axis=1)
    return ctab, stab


def kernel(x_prompt, x_sample, meta_tokens, norm1_g, w_in, q_norm_g, k_norm_g, hy_conv_w, hy_conv_b, hy_ffn_w1, hy_ffn_b1, hy_sin_freq, hy_ffn_w2, hy_ffn_b2, hy_ffn_w3, hy_decay, hy_dskip, attn_out_g, hy_out_g, w_out, norm2_g, peer_wq, peer_keys, peer_u, peer_v, final_g):
    depth, d, _ = w_in.shape
    b_p, s, _ = x_prompt.shape
    b_s = x_sample.shape[0]
    assert x_sample.shape[1] == s and s % LANE == 0 and s % GRID_W == 0
    nb = b_p + b_s
    seq_l = s + N_META
    lp = s + LANE
    nj = lp // LANE
    t = nb * lp
    n_heads = attn_out_g.shape[1]
    aw = n_heads * HEAD_DIM
    hw = d - aw
    kw = (w_in.shape[2] - aw - 3 * hw) // 2
    n_kv = kw // HEAD_DIM
    assert kw == LANE, "k/v lane-pair layout assumes two kv heads"
    peer_heads = peer_keys.shape[2]
    nkeys = peer_keys.shape[3]
    assert nkeys == LANE

    x = jnp.concatenate([x_prompt, x_sample], axis=0)
    front = jnp.concatenate([jnp.zeros((PADF, d), f32), meta_tokens.astype(f32)], axis=0)
    h = jnp.concatenate([jnp.broadcast_to(front[None], (nb, LANE, d)), x], axis=1).reshape(t, d)

    bd_qk = _block_diag(aw + kw)
    bd_hy = _block_diag(hw)
    swap_q = jnp.arange(aw) ^ 1
    swap_k = jnp.arange(kw) ^ 1
    bands = jnp.linspace(1e-4, HY_BANDS - 1, HY_BANDS, dtype=f32)
    bands_p = jnp.zeros((1, LANE), f32).at[0, :HY_BANDS].set(bands).at[0, HY_BANDS:2 * HY_BANDS].set(bands)

    for l in range(depth):
        w = w_in[l]
        wq, wk = w[:, :aw], w[:, aw:aw + kw]
        wv, wz = w[:, aw + kw:aw + 2 * kw], w[:, aw + 2 * kw:]
        w_ext = jnp.concatenate([wq, wk, wq[:, swap_q], wk[:, swap_k], wv, wz], axis=1).astype(bf16)
        ctab, stab = _rope_tables(s, lp, q_norm_g[l], k_norm_g[l], n_heads, n_kv)
        q, kx, vx, z = _inproj(h, norm1_g[l][None, :], w_ext, bd_qk, ctab, stab,
                               nb=nb, lp=lp, qw=aw, kw=kw, zw=3 * hw)
        o_att = _attention(q, kx, vx, attn_out_g[l].reshape(1, aw), nb=nb, lp=lp, n_kv=n_kv)

        w1 = hy_ffn_w1[l]
        fh = w1.shape[1]
        w1p = jnp.zeros((LANE, fh), f32).at[:2 * HY_BANDS].set(w1[1:]).at[2 * HY_BANDS].set(w1[0])
        filt = _hyfilter(bands_p, w1p, hy_ffn_b1[l][None], hy_sin_freq[l], hy_ffn_w2[l], hy_ffn_b2[l][None],
                         hy_ffn_w3[l], hy_decay[l].reshape(1, 2 * hw), lp=lp, seq_l=seq_l)
        hf, hb = filt[:seq_l, :hw], filt[:seq_l, hw:]
        lags = jnp.concatenate([jnp.zeros((lp - seq_l + 1, hw), f32), hb[:0:-1], hf,
                                jnp.zeros((lp - seq_l, hw), f32)], axis=0)
        hc = lags.T.reshape(hw, 2 * nj, LANE)

        u, x1 = _hypre(z, hy_conv_w[l], hy_conv_b[l][None], nb=nb, lp=lp, hw=hw)
        ut = u.reshape(nb, nj, LANE, hw).transpose(3, 1, 0, 2).reshape(hw, nj * nb, LANE)
        yt = _hyconv(ut, hc, nj=nj, nb=nb)
        y = yt.reshape(hw, nj, nb, LANE).transpose(2, 1, 3, 0).reshape(t, hw)

        h = _outproj(h, o_att, y, u, x1, hy_dskip[l][None], hy_out_g[l].reshape(1, hw), bd_hy,
                     w_out[l].astype(bf16))

        dh = peer_keys.shape[4]
        keys = peer_keys[l].transpose(1, 0, 2, 3).reshape(2 * peer_heads, nkeys, dh).astype(bf16)
        h = _peer(h, norm2_g[l][None], peer_wq[l].T.astype(bf16), keys,
                  peer_u[l].astype(bf16), peer_v[l].astype(bf16), nheads=peer_heads)

    h3 = h.reshape(nb, lp, d)
    y_prompt = _final(h3, final_g[None], b0=0, nbatch=b_p, s=s)
    y_sample = _final(h3, final_g[None], b0=b_p, nbatch=b_s, s=s)
    return (y_prompt, y_sample)
```
